```python
import functools
import jax, jax.numpy as jnp
from jax import lax
import numpy as np

D_MODEL = 2048
BATCH = 2
SEQ = 8192
DEPTH = 1
DEC_BATCH = 32
DEC_SEQ = 4
PAST_LEN = 16384
PAGE_SIZE = 128

N_MEM = 256
SB_DIM = 128
SB_HEADS = D_MODEL // 256
GDN_DK = 128
GDN_DV = 128
GDN_HEADS = D_MODEL // 256
CONV_W = 4
MEM_HEADS = 4
MEM_DIM = D_MODEL // 8
D_FF = ((8 * D_MODEL + 3 * 256 - 1) // (3 * 256)) * 256
SB_BLOCK = 128
GDN_CHUNK = 64
EPS = 1e-6

SB_W = SB_HEADS * SB_DIM
GDN_KW = GDN_HEADS * GDN_DK
GDN_VW = GDN_HEADS * GDN_DV
MEM_W = MEM_HEADS * MEM_DIM
CONV_CH = 2 * GDN_KW + GDN_VW
IN_SIZES = (SB_W, SB_W, SB_W, CONV_CH, GDN_VW, GDN_HEADS, GDN_HEADS, MEM_W, 3 * D_MODEL)
IN_OFFSETS = tuple(int(s) for s in np.cumsum(IN_SIZES)[:-1])
IN_W = int(sum(IN_SIZES))

kernel_name = 'stickbreak_gdn_memory_hybrid_step'


def rms_norm(x, g):
    x32 = x.astype(jnp.float32)
    y = x32 * lax.rsqrt(jnp.mean(x32 * x32, axis=-1, keepdims=True) + EPS)
    return (y * g.astype(jnp.float32)).astype(x.dtype)


def l2_normalize(x):
    x32 = x.astype(jnp.float32)
    return x32 * lax.rsqrt(jnp.sum(x32 * x32, axis=-1, keepdims=True) + EPS)


def sb_weights(z, mask):
    log_keep = jnp.where(mask, jax.nn.log_sigmoid(-z), 0.0)
    after = lax.cumsum(log_keep, axis=z.ndim - 1, reverse=True) - log_keep
    return jnp.where(mask, jnp.exp(jax.nn.log_sigmoid(z) + after), 0.0)


def sb_prompt(q, k, v, bias):
    B, S, H, d = q.shape
    nb = S // SB_BLOCK
    key_pos = jnp.arange(S)
    qb = jnp.moveaxis(q.reshape(B, nb, SB_BLOCK, H, d), 1, 0)
    b32 = bias.astype(jnp.float32)[None, :, None, None]

    def block(args):
        q_blk, start = args
        z = jnp.einsum('bqhd,bkhd->bhqk', q_blk, k).astype(jnp.float32) * (d ** -0.5) + b32
        q_pos = start + jnp.arange(SB_BLOCK)
        mask = key_pos[None, :] < q_pos[:, None]
        a = sb_weights(z, mask).astype(v.dtype)
        return jnp.einsum('bhqk,bkhd->bqhd', a, v)

    o = lax.map(block, (qb, jnp.arange(nb) * SB_BLOCK))
    return jnp.moveaxis(o, 0, 1).reshape(B, S, H * d)


def sb_sample(q, k_new, v_new, bias, cache_k, cache_v, page_table):
    Bd, Q, H, d = q.shape
    k_past = cache_k[page_table].reshape(Bd, -1, H, d)
    v_past = cache_v[page_table].reshape(Bd, -1, H, d)
    P = k_past.shape[1]
    z = jnp.concatenate([jnp.einsum('bqhd,bkhd->bhqk', q, k_past),
                         jnp.einsum('bqhd,bkhd->bhqk', q, k_new)], axis=-1)
    z = z.astype(jnp.float32) * (d ** -0.5) + bias.astype(jnp.float32)[None, :, None, None]
    key_pos = jnp.arange(P + Q)
    q_pos = P + jnp.arange(Q)
    mask = key_pos[None, :] < q_pos[:, None]
    a = sb_weights(z, mask).astype(v_new.dtype)
    o = (jnp.einsum('bhqk,bkhd->bqhd', a[..., :P], v_past)
         + jnp.einsum('bhqk,bkhd->bqhd', a[..., P:], v_new))
    return o.reshape(Bd, Q, H * d)


def causal_conv(x, buf, w):
    T = x.shape[1]
    xp = jnp.concatenate([buf.astype(x.dtype), x], axis=1)
    y = xp[:, 0:T] * w[0]
    for i in range(1, CONV_W):
        y = y + xp[:, i:i + T] * w[i]
    return jax.nn.silu(y), xp[:, T:]


def gated_delta_chunked(q, k, v, g, beta, s0, chunk):
    B, T, H, dk = k.shape
    dv = v.shape[-1]
    n = T // chunk

    def blocks(t):
        t = t.reshape((B, n, chunk, H) + t.shape[3:])
        return jnp.moveaxis(jnp.moveaxis(t, 1, 0), 3, 2)

    qc, kc, vc, bc = blocks(q), blocks(k), blocks(v), blocks(beta)
    gc = jnp.cumsum(blocks(g), axis=-1)
    incl = jnp.tril(jnp.ones((chunk, chunk), dtype=bool))
    strict = jnp.tril(jnp.ones((chunk, chunk), dtype=bool), -1)
    diff = gc[..., :, None] - gc[..., None, :]
    decay = jnp.where(incl, jnp.exp(jnp.where(incl, diff, 0.0)), 0.0)
    kb = kc * bc[..., None]
    m = jnp.where(strict, jnp.einsum('nbhid,nbhjd->nbhij', kb, kc) * decay, 0.0)
    eye = jnp.eye(chunk, dtype=m.dtype)
    rhs = jnp.concatenate([vc * bc[..., None], kb * jnp.exp(gc)[..., None]], axis=-1)
    sol = lax.linalg.triangular_solve(m + eye, rhs, left_side=True, lower=True)
    u, w = sol[..., :dv], sol[..., dv:]
    qk = jnp.where(incl, jnp.einsum('nbhid,nbhjd->nbhij', qc, kc) * decay, 0.0)
    q_dec = qc * jnp.exp(gc)[..., None]
    k_dec = kc * jnp.exp(gc[..., -1:] - gc)[..., None]
    g_tot = jnp.exp(gc[..., -1])

    def step(s, xs):
        u_i, w_i, qk_i, qd_i, kd_i, gt_i = xs
        v_new = u_i - jnp.einsum('bhck,bhkv->bhcv', w_i, s)
        o_i = jnp.einsum('bhck,bhkv->bhcv', qd_i, s) + jnp.einsum('bhij,bhjv->bhiv', qk_i, v_new)
        s = s * gt_i[..., None, None] + jnp.einsum('bhck,bhcv->bhkv', kd_i, v_new)
        return s, o_i

    s_fin, o = lax.scan(step, s0.astype(jnp.float32), (u, w, qk, q_dec, k_dec, g_tot))
    o = jnp.swapaxes(jnp.moveaxis(o, 0, 1), 2, 3).reshape(B, T, H, dv)
    return o, s_fin


def gdn_mixer(qkv_raw, z, b_raw, a_raw, conv_buf, s0, w_conv, a_log, dt_bias, gdn_norm, chunk):
    B, T, _ = qkv_raw.shape
    qkv, new_buf = causal_conv(qkv_raw, conv_buf, w_conv)
    q, k, v = jnp.split(qkv, (GDN_KW, 2 * GDN_KW), axis=-1)
    q = l2_normalize(q.reshape(B, T, GDN_HEADS, GDN_DK)) * (GDN_DK ** -0.5)
    k = l2_normalize(k.reshape(B, T, GDN_HEADS, GDN_DK))
    v = v.reshape(B, T, GDN_HEADS, GDN_DV).astype(jnp.float32)
    beta = jax.nn.sigmoid(b_raw.astype(jnp.float32))
    g = -jnp.exp(a_log.astype(jnp.float32)) * jax.nn.softplus(
        a_raw.astype(jnp.float32) + dt_bias.astype(jnp.float32))
    o, s_new = gated_delta_chunked(q, k, v, g, beta, s0, chunk)
    o = rms_norm(o, gdn_norm) * jax.nn.silu(z.reshape(B, T, GDN_HEADS, GDN_DV).astype(jnp.float32))
    return o.reshape(B, T, GDN_VW).astype(qkv_raw.dtype), s_new.astype(s0.dtype), new_buf


def mem_kv(mem, ln_mem, w_mem_kv, k_norm):
    B, N, _ = mem.shape
    k, v = jnp.split(rms_norm(mem, ln_mem) @ w_mem_kv, 2, axis=-1)
    k = rms_norm(k.reshape(B, N, MEM_HEADS, MEM_DIM), k_norm)
    return k, v.reshape(B, N, MEM_HEADS, MEM_DIM)


def mem_attend(q_raw, mem_k, mem_v, q_norm):
    B, T, _ = q_raw.shape
    q = rms_norm(q_raw.reshape(B, T, MEM_HEADS, MEM_DIM), q_norm)
    s = jnp.einsum('bthd,bnhd->bhtn', q, mem_k).astype(jnp.float32) * (MEM_DIM ** -0.5)
    p = jax.nn.softmax(s, axis=-1).astype(mem_v.dtype)
    return jnp.einsum('bhtn,bnhd->bthd', p, mem_v).reshape(B, T, MEM_W)


def trunk_layer(x, p, sb_attend, conv_buf, s0, mem_k, mem_v, chunk):
    B, T, _ = x.shape
    h = rms_norm(x, p['ln_mix'])
    proj = h @ p['w_in']
    sb_q, sb_k, sb_v, gdn_qkv, gdn_z, gdn_b, gdn_a, mem_q, gates = jnp.split(proj, IN_OFFSETS, axis=-1)
    sb_k = sb_k.reshape(B, T, SB_HEADS, SB_DIM)
    sb_v = sb_v.reshape(B, T, SB_HEADS, SB_DIM)
    o_sb = sb_attend(sb_q.reshape(B, T, SB_HEADS, SB_DIM), sb_k, sb_v, p['sb_bias'])
    o_gdn, s_new, buf_new = gdn_mixer(gdn_qkv, gdn_z, gdn_b, gdn_a, conv_buf, s0, p['w_conv'],
                                      p['a_log'], p['dt_bias'], p['gdn_norm'], chunk)
    o_mem = mem_attend(mem_q, mem_k, mem_v, p['q_norm_mem'])
    gt = jax.nn.sigmoid(gates.astype(jnp.float32)).astype(x.dtype)
    g_sb, g_gdn, g_mem = jnp.split(gt, 3, axis=-1)
    merged = (g_sb * (o_sb @ p['w_proj_sb']) + g_gdn * (o_gdn @ p['w_proj_gdn'])
              + g_mem * (o_mem @ p['w_proj_mem']))
    x = x + merged @ p['w_out']
    f_gate, f_up = jnp.split(rms_norm(x, p['ln_ffn']) @ p['w_ffn_in'], 2, axis=-1)
    x = x + (jax.nn.silu(f_gate) * f_up) @ p['w_ffn_out']
    return x, sb_k, sb_v, s_new, buf_new


def setup_inputs(seed: int = 0) -> dict:
    key = jax.random.key(seed)
    ks = jax.random.split(key, 32)
    n_pages = PAST_LEN // PAGE_SIZE
    n_used = DEC_BATCH * n_pages
    n_phys = n_used + max(n_used // 4, 1)
    f32 = jnp.float32

    def nrm(k, shape, scale=1.0):
        return jax.random.normal(k, shape, f32) * scale

    def gain(k, shape):
        return 1.0 + 0.02 * jax.random.normal(k, shape, f32)

    page_table = jax.random.permutation(ks[0], n_phys)[:n_used].reshape(DEC_BATCH, n_pages).astype(jnp.int32)
    dt = jnp.exp(jax.random.uniform(ks[1], (DEPTH, GDN_HEADS), f32, np.log(1e-3), np.log(1e-1)))
    return {
        'x_prompt': nrm(ks[2], (BATCH, SEQ, D_MODEL)),
        'x_sample': nrm(ks[3], (DEC_BATCH, DEC_SEQ, D_MODEL)),
        'mem_prompt': nrm(ks[4], (BATCH, N_MEM, D_MODEL)),
        'cache_sb_k': nrm(ks[5], (DEPTH, n_phys, PAGE_SIZE, SB_HEADS, SB_DIM)),
        'cache_sb_v': nrm(ks[6], (DEPTH, n_phys, PAGE_SIZE, SB_HEADS, SB_DIM)),
        'state_gdn': nrm(ks[7], (DEPTH, DEC_BATCH, GDN_HEADS, GDN_DK, GDN_DV), 0.3),
        'state_conv': nrm(ks[8], (DEPTH, DEC_BATCH, CONV_W - 1, CONV_CH)),
        'cache_mem_k': nrm(ks[9], (DEPTH, DEC_BATCH, N_MEM, MEM_HEADS, MEM_DIM)),
        'cache_mem_v': nrm(ks[10], (DEPTH, DEC_BATCH, N_MEM, MEM_HEADS, MEM_DIM)),
        'page_table': page_table,
        'ln_mix': gain(ks[11], (DEPTH, D_MODEL)),
        'w_in': nrm(ks[12], (DEPTH, D_MODEL, IN_W), D_MODEL ** -0.5),
        'sb_bias': jax.random.uniform(ks[27], (DEPTH, SB_HEADS), f32, -8.0, -6.0),
        'w_conv': nrm(ks[13], (DEPTH, CONV_W, CONV_CH), CONV_W ** -0.5),
        'a_log': jnp.log(jax.random.uniform(ks[14], (DEPTH, GDN_HEADS), f32, 1.0, 16.0)),
        'dt_bias': dt + jnp.log(-jnp.expm1(-dt)),
        'gdn_norm': gain(ks[15], (DEPTH, GDN_DV)),
        'q_norm_mem': gain(ks[16], (DEPTH, MEM_DIM)),
        'k_norm_mem': gain(ks[17], (DEPTH, MEM_DIM)),
        'ln_mem': gain(ks[18], (DEPTH, D_MODEL)),
        'w_mem_kv': nrm(ks[19], (DEPTH, D_MODEL, 2 * MEM_W), D_MODEL ** -0.5),
        'w_proj_sb': nrm(ks[20], (DEPTH, SB_W, D_MODEL), SB_W ** -0.5),
        'w_proj_gdn': nrm(ks[21], (DEPTH, GDN_VW, D_MODEL), GDN_VW ** -0.5),
        'w_proj_mem': nrm(ks[22], (DEPTH, MEM_W, D_MODEL), MEM_W ** -0.5),
        'w_out': nrm(ks[23], (DEPTH, D_MODEL, D_MODEL), D_MODEL ** -0.5),
        'ln_ffn': gain(ks[24], (DEPTH, D_MODEL)),
        'w_ffn_in': nrm(ks[25], (DEPTH, D_MODEL, 2 * D_FF), D_MODEL ** -0.5),
        'w_ffn_out': nrm(ks[26], (DEPTH, D_FF, D_MODEL), D_FF ** -0.5),
    }


def reference(x_prompt, x_sample, mem_prompt, cache_sb_k, cache_sb_v, state_gdn, state_conv,
              cache_mem_k, cache_mem_v, page_table, ln_mix, w_in, sb_bias, w_conv, a_log, dt_bias,
              gdn_norm, q_norm_mem, k_norm_mem, ln_mem, w_mem_kv, w_proj_sb, w_proj_gdn, w_proj_mem,
              w_out, ln_ffn, w_ffn_in, w_ffn_out):
    yp, ys = x_prompt, x_sample
    bp, tp = x_prompt.shape[0], x_prompt.shape[1]
    ts = x_sample.shape[1]
    kp_l, vp_l, ks_l, vs_l, sp_l, ss_l, cp_l, cs_l, mk_l, mv_l = ([] for _ in range(10))
    for l in range(DEPTH):
        p = {'ln_mix': ln_mix[l], 'w_in': w_in[l], 'sb_bias': sb_bias[l], 'w_conv': w_conv[l],
             'a_log': a_log[l], 'dt_bias': dt_bias[l], 'gdn_norm': gdn_norm[l],
             'q_norm_mem': q_norm_mem[l], 'w_proj_sb': w_proj_sb[l], 'w_proj_gdn': w_proj_gdn[l],
             'w_proj_mem': w_proj_mem[l], 'w_out': w_out[l], 'ln_ffn': ln_ffn[l],
             'w_ffn_in': w_ffn_in[l], 'w_ffn_out': w_ffn_out[l]}
        mk, mv = mem_kv(mem_prompt, ln_mem[l], w_mem_kv[l], k_norm_mem[l])
        buf0 = jnp.zeros((bp, CONV_W - 1, CONV_CH), x_prompt.dtype)
        s0 = jnp.zeros((bp, GDN_HEADS, GDN_DK, GDN_DV), x_prompt.dtype)
        yp, kp, vp, sp, cp = trunk_layer(yp, p, sb_prompt, buf0, s0, mk, mv, min(GDN_CHUNK, tp))
        attend = functools.partial(sb_sample, cache_k=cache_sb_k[l], cache_v=cache_sb_v[l],
                                   page_table=page_table)
        ys, kn, vn, sn, cn = trunk_layer(ys, p, attend, state_conv[l], state_gdn[l],
                                         cache_mem_k[l], cache_mem_v[l], ts)
        kp_l.append(kp); vp_l.append(vp); ks_l.append(kn); vs_l.append(vn)
        sp_l.append(sp); ss_l.append(sn); cp_l.append(cp); cs_l.append(cn)
        mk_l.append(mk); mv_l.append(mv)
    new_sb_k_prompt = jnp.stack(kp_l)
    new_sb_v_prompt = jnp.stack(vp_l)
    new_sb_k_sample = jnp.stack(ks_l)
    new_sb_v_sample = jnp.stack(vs_l)
    new_gdn_prompt = jnp.stack(sp_l)
    new_gdn_sample = jnp.stack(ss_l)
    new_conv_prompt = jnp.stack(cp_l)
    new_conv_sample = jnp.stack(cs_l)
    new_mem_k_prompt = jnp.stack(mk_l)
    new_mem_v_prompt = jnp.stack(mv_l)
    return (yp, ys, new_sb_k_prompt, new_sb_v_prompt, new_sb_k_sample, new_sb_v_sample,
            new_gdn_prompt, new_gdn_sample, new_conv_prompt, new_conv_sample,
            new_mem_k_prompt, new_mem_v_prompt)
```

```python
import functools

import jax
import jax.numpy as jnp
from jax import lax
from jax.experimental import pallas as pl
from jax.experimental.pallas import tpu as pltpu

F32 = jnp.float32
BF16 = jnp.bfloat16
HIGHEST = lax.Precision.HIGHEST

EPS = 1e-6
SB_DIM = 128
GDN_DK = 128
GDN_DV = 128
GDN_CHUNK = 64
CONV_W = 4
SUBLANES = 8
LANES = 128
SAMPLE_PAD = 2 * SUBLANES
VMEM_LIMIT = 56 * 1024 * 1024


def _params(*sem):
    return pltpu.CompilerParams(dimension_semantics=sem, vmem_limit_bytes=VMEM_LIMIT)


def _tile(n, pref):
    if n <= pref:
        return n
    t = pref
    while n % t:
        t //= 2
    return t


def _dot(a, b, precise=False):
    if precise:
        return jnp.dot(a.astype(F32), b.astype(F32), precision=HIGHEST, preferred_element_type=F32)
    return jnp.dot(a.astype(BF16), b.astype(BF16), preferred_element_type=F32)


def _dot_nt(a, b, precise=False):
    dims = (((1,), (1,)), ((), ()))
    if precise:
        return lax.dot_general(a.astype(F32), b.astype(F32), dims, precision=HIGHEST, preferred_element_type=F32)
    return lax.dot_general(a.astype(BF16), b.astype(BF16), dims, preferred_element_type=F32)


def _dot_tn(a, b, precise=False):
    dims = (((0,), (0,)), ((), ()))
    if precise:
        return lax.dot_general(a.astype(F32), b.astype(F32), dims, precision=HIGHEST, preferred_element_type=F32)
    return lax.dot_general(a.astype(BF16), b.astype(BF16), dims, preferred_element_type=F32)


def _sigmoid(x):
    return 1.0 / (1.0 + jnp.exp(-x))


def _softplus(x):
    return jnp.maximum(x, 0.0) + jnp.log(1.0 + jnp.exp(-jnp.abs(x)))


def _rmsnorm_body(x_ref, g_ref, o_ref):
    x = x_ref[...].astype(F32)
    ms = jnp.mean(x * x, axis=-1, keepdims=True)
    o_ref[...] = (x * lax.rsqrt(ms + EPS) * g_ref[...]).astype(o_ref.dtype)


def _rmsnorm(x, g, out_dtype=BF16):
    m, d = x.shape
    tm = _tile(m, 512)
    return pl.pallas_call(
        _rmsnorm_body,
        out_shape=jax.ShapeDtypeStruct((m, d), out_dtype),
        grid=(m // tm,),
        in_specs=[pl.BlockSpec((tm, d), lambda i: (i, 0)), pl.BlockSpec((1, d), lambda i: (0, 0))],
        out_specs=pl.BlockSpec((tm, d), lambda i: (i, 0)),
        compiler_params=_params("parallel"),
        name="rmsnorm",
    )(x, g.reshape(1, d).astype(F32))


def _headnorm_body(x_ref, g_ref, o_ref, *, heads, dh):
    for h in range(heads):
        x = x_ref[:, h * dh:(h + 1) * dh].astype(F32)
        ms = jnp.mean(x * x, axis=-1, keepdims=True)
        o_ref[:, h * dh:(h + 1) * dh] = (x * lax.rsqrt(ms + EPS) * g_ref[...]).astype(o_ref.dtype)


def _headnorm(x, g, heads, col_block):
    m = x.shape[0]
    dh = g.shape[-1]
    w = heads * dh
    tm = _tile(m, 512)
    return pl.pallas_call(
        functools.partial(_headnorm_body, heads=heads, dh=dh),
        out_shape=jax.ShapeDtypeStruct((m, w), F32),
        grid=(m // tm,),
        in_specs=[pl.BlockSpec((tm, w), lambda i: (i, col_block)), pl.BlockSpec((1, dh), lambda i: (0, 0))],
        out_specs=pl.BlockSpec((tm, w), lambda i: (i, 0)),
        compiler_params=_params("parallel"),
        name="headnorm",
    )(x, g.reshape(1, dh).astype(F32))


def _mm_body(a_ref, b_ref, o_ref):
    o_ref[...] = jnp.dot(a_ref[...], b_ref[...], preferred_element_type=F32).astype(o_ref.dtype)


def _mm_res_body(a_ref, b_ref, r_ref, o_ref):
    o_ref[...] = (r_ref[...] + jnp.dot(a_ref[...], b_ref[...], preferred_element_type=F32)).astype(o_ref.dtype)


def _matmul(a, b, res=None, out_dtype=F32):
    m, k = a.shape
    n = b.shape[1]
    tm = _tile(m, 1024)
    tn = _tile(n, 512)
    in_specs = [pl.BlockSpec((tm, k), lambda i, j: (i, 0)), pl.BlockSpec((k, tn), lambda i, j: (0, j))]
    args = [a, b]
    body = _mm_body
    if res is not None:
        in_specs.append(pl.BlockSpec((tm, tn), lambda i, j: (i, j)))
        args.append(res)
        body = _mm_res_body
    return pl.pallas_call(
        body,
        out_shape=jax.ShapeDtypeStruct((m, n), out_dtype),
        grid=(m // tm, n // tn),
        in_specs=in_specs,
        out_specs=pl.BlockSpec((tm, tn), lambda i, j: (i, j)),
        compiler_params=_params("parallel", "arbitrary"),
        name="matmul",
    )(*args)


def _sb_tile(q, k, v, uo, carry, bias, mask, scale):
    r = q.shape[0]
    tk = k.shape[0]
    z = _dot_nt(q, k) * scale + bias
    sp = _softplus(z)
    log_keep = -sp
    log_hit = z - sp
    if mask is not None:
        log_keep = jnp.where(mask, log_keep, 0.0)
    hi = log_keep.astype(BF16).astype(F32)
    lo = log_keep - hi
    sums = jnp.dot(jnp.concatenate([hi, lo], axis=0).astype(BF16), uo, preferred_element_type=F32)
    sums = sums[:r] + sums[r:]
    after = sums[:, :tk]
    total = sums[:, tk:]
    a = jnp.exp(log_hit + after + carry)
    if mask is not None:
        a = jnp.where(mask, a, 0.0)
    return _dot(a, v), carry + total


def _suffix_sum_operand(tk):
    j = lax.broadcasted_iota(jnp.int32, (tk, 2 * tk), 0)
    s = lax.broadcasted_iota(jnp.int32, (tk, 2 * tk), 1)
    return jnp.where((s >= tk) | (j > s), 1.0, 0.0).astype(BF16)


def _sb_prompt_body(bias_ref, q_ref, k_ref, v_ref, uo_ref, o_ref, kb_ref, vb_ref, *, tq, tk, scale):
    h = pl.program_id(1)
    qi = pl.program_id(2)

    @pl.when(qi == 0)
    def _():
        kb_ref[...] = k_ref[...].astype(BF16)
        vb_ref[...] = v_ref[...].astype(BF16)

    q = q_ref[...].astype(BF16)
    bias = bias_ref[h]
    uo = uo_ref[...]
    n_blocks = (qi + 1) * (tq // tk)
    q_pos = qi * tq + lax.broadcasted_iota(jnp.int32, (tq, tk), 0)
    k_off = lax.broadcasted_iota(jnp.int32, (tq, tk), 1)

    def body(jj, state):
        acc, carry = state
        start = pl.multiple_of((n_blocks - 1 - jj) * tk, tk)
        k = kb_ref[pl.ds(start, tk), :]
        v = vb_ref[pl.ds(start, tk), :]
        mask = (start + k_off) < q_pos
        contrib, carry = _sb_tile(q, k, v, uo, carry, bias, mask, scale)
        return acc + contrib, carry

    zeros = jnp.zeros((tq, tk), F32)
    acc, _ = lax.fori_loop(0, n_blocks, body, (jnp.zeros((tq, q.shape[1]), F32), zeros))
    o_ref[...] = acc.astype(o_ref.dtype)


def _sb_prompt(proj, bias, batch, seq, heads):
    d = SB_DIM
    tq = tk = 128
    nq = seq // tq
    return pl.pallas_call(
        functools.partial(_sb_prompt_body, tq=tq, tk=tk, scale=d ** -0.5),
        out_shape=jax.ShapeDtypeStruct((batch * seq, heads * d), BF16),
        grid=(batch, heads, nq),
        in_specs=[
            pl.BlockSpec(memory_space=pltpu.SMEM),
            pl.BlockSpec((tq, d), lambda b, h, i: (b * nq + i, h)),
            pl.BlockSpec((seq, d), lambda b, h, i: (b, heads + h)),
            pl.BlockSpec((seq, d), lambda b, h, i: (b, 2 * heads + h)),
            pl.BlockSpec((tk, 2 * tk), lambda b, h, i: (0, 0)),
        ],
        out_specs=pl.BlockSpec((tq, d), lambda b, h, i: (b * nq + i, h)),
        scratch_shapes=[pltpu.VMEM((seq, d), BF16), pltpu.VMEM((seq, d), BF16)],
        compiler_params=_params("parallel", "parallel", "arbitrary"),
        name="sb_prompt",
    )(bias.astype(F32), proj, proj, proj, _suffix_sum_operand(tk))


def _sb_sample_body(pt_ref, bias_ref, q_ref, kn_ref, vn_ref, uo_ref, *refs, heads, pages_per_step, page, scale):
    del pt_ref
    page_refs = refs[:2 * pages_per_step]
    o_ref, acc_ref, carry_ref = refs[2 * pages_per_step:]
    s = pl.program_id(1)
    d = SB_DIM
    qp = q_ref.shape[0]
    q = q_ref[...].astype(BF16)
    uo = uo_ref[...]

    @pl.when(s == 0)
    def _():
        pad = jnp.zeros((page - qp, heads * d), F32)
        k_new = jnp.concatenate([kn_ref[...], pad], axis=0).astype(BF16)
        v_new = jnp.concatenate([vn_ref[...], pad], axis=0).astype(BF16)
        mask = lax.broadcasted_iota(jnp.int32, (qp, page), 1) < lax.broadcasted_iota(jnp.int32, (qp, page), 0)
        for h in range(heads):
            hs = slice(h * d, (h + 1) * d)
            contrib, carry = _sb_tile(q[:, hs], k_new[:, hs], v_new[:, hs], uo,
                                      jnp.zeros((qp, page), F32), bias_ref[h], mask, scale)
            acc_ref[:, hs] = contrib
            carry_ref[:, hs] = carry

    for i in range(pages_per_step):
        k_page = page_refs[2 * i][...].astype(BF16)
        v_page = page_refs[2 * i + 1][...].astype(BF16)
        for h in range(heads):
            hs = slice(h * d, (h + 1) * d)
            contrib, carry = _sb_tile(q[:, hs], k_page[:, hs], v_page[:, hs], uo,
                                      carry_ref[:, hs], bias_ref[h], None, scale)
            acc_ref[:, hs] += contrib
            carry_ref[:, hs] = carry

    @pl.when(s == pl.num_programs(1) - 1)
    def _():
        o_ref[...] = acc_ref[...].astype(o_ref.dtype)


def _sb_sample(proj, bias, cache_k, cache_v, page_table, layer, heads):
    d = SB_DIM
    batch, n_pages = page_table.shape
    depth, n_phys, page = cache_k.shape[:3]
    assert page == d, "one cached page is one key block of the suffix-sum operand"
    w = heads * d
    qp = SAMPLE_PAD
    pps = _tile(n_pages, 8)
    steps = n_pages // pps
    ck = cache_k.reshape(depth * n_phys, page, w)
    cv = cache_v.reshape(depth * n_phys, page, w)

    def page_spec(i):
        def index(b, s, pt):
            return (layer * n_phys + pt[b * n_pages + (n_pages - 1 - (s * pps + i))], 0, 0)
        return pl.BlockSpec((None, page, w), index)

    page_specs = []
    page_args = []
    for i in range(pps):
        page_specs += [page_spec(i), page_spec(i)]
        page_args += [ck, cv]
    grid_spec = pltpu.PrefetchScalarGridSpec(
        num_scalar_prefetch=1,
        grid=(batch, steps),
        in_specs=[
            pl.BlockSpec(memory_space=pltpu.SMEM),
            pl.BlockSpec((qp, w), lambda b, s, pt: (b, 0)),
            pl.BlockSpec((qp, w), lambda b, s, pt: (b, 1)),
            pl.BlockSpec((qp, w), lambda b, s, pt: (b, 2)),
            pl.BlockSpec((page, 2 * page), lambda b, s, pt: (0, 0)),
        ] + page_specs,
        out_specs=pl.BlockSpec((qp, w), lambda b, s, pt: (b, 0)),
        scratch_shapes=[pltpu.VMEM((qp, w), F32), pltpu.VMEM((qp, w), F32)],
    )
    return pl.pallas_call(
        functools.partial(_sb_sample_body, heads=heads, pages_per_step=pps, page=page, scale=d ** -0.5),
        out_shape=jax.ShapeDtypeStruct((batch * qp, w), BF16),
        grid_spec=grid_spec,
        compiler_params=_params("parallel", "arbitrary"),
        name="sb_sample",
    )(page_table.reshape(-1).astype(jnp.int32), bias.astype(F32), proj, proj, proj,
      _suffix_sum_operand(page), *page_args)


def _unit_lower_inverse(m, precise_dot):
    c = m.shape[0]
    eye = (lax.broadcasted_iota(jnp.int32, (c, c), 0) == lax.broadcasted_iota(jnp.int32, (c, c), 1)).astype(F32)
    p = -m
    inv = eye + p
    span = 2
    while span < c:
        p = precise_dot(p, p)
        inv = inv + precise_dot(inv, p)
        span *= 2
    return inv


def _gdn_body(qkv_ref, z_ref, ba_ref, bat_ref, prev_ref, s0_ref, wconv_ref, arow_ref, dtrow_ref,
              acol_ref, dtcol_ref, gnorm_ref, o_ref, s_ref, tail_ref, *, heads, chunk, n_valid, precise):
    t = pl.program_id(1)
    tt = qkv_ref.shape[0]
    kw = heads * GDN_DK
    sub = SUBLANES
    pdot = functools.partial(_dot, precise=True)

    @pl.when(t == 0)
    def _():
        tail_ref[...] = prev_ref[...]
        s_ref[...] = s0_ref[...]

    x = qkv_ref[...]
    head_rows = jnp.concatenate([tail_ref[...], x[:sub]], axis=0)
    y = x * wconv_ref[CONV_W - 1:CONV_W, :]
    for shift in range(1, CONV_W):
        first = pltpu.roll(head_rows, shift, 0)[sub:]
        if tt > sub:
            shifted = jnp.concatenate([first, pltpu.roll(x, shift, 0)[sub:]], axis=0)
        else:
            shifted = first
        y = y + shifted * wconv_ref[CONV_W - 1 - shift:CONV_W - shift, :]
    tail_ref[...] = x[tt - sub:]
    y = y * _sigmoid(y)

    tp = max(tt, chunk)

    def rows(a, c):
        if tt < chunk:
            return jnp.concatenate([a, jnp.zeros((chunk - tt, a.shape[1]), a.dtype)], axis=0)
        return a[c * chunk:(c + 1) * chunk]

    ri = lax.broadcasted_iota(jnp.int32, (chunk, chunk), 0)
    ci = lax.broadcasted_iota(jnp.int32, (chunk, chunk), 1)
    incl = ri >= ci
    strict = ri > ci
    tri = incl.astype(F32)
    tri_t = (ri <= ci).astype(F32)

    ba_all = ba_ref[...]
    bat_all = bat_ref[...]
    z_all = z_ref[...]
    for c in range(tp // chunk):
        ba = rows(ba_all, c)
        g_tok = -jnp.exp(arow_ref[...]) * _softplus(ba + dtrow_ref[...])
        beta_tok = _sigmoid(ba)
        bat = bat_all[:, c * chunk:(c + 1) * chunk]
        g_row = -jnp.exp(acol_ref[...]) * _softplus(bat + dtcol_ref[...])
        if n_valid is not None:
            valid_r = lax.broadcasted_iota(jnp.int32, g_tok.shape, 0) < n_valid
            g_tok = jnp.where(valid_r, g_tok, 0.0)
            beta_tok = jnp.where(valid_r, beta_tok, 0.0)
            g_row = jnp.where(lax.broadcasted_iota(jnp.int32, g_row.shape, 1) < n_valid, g_row, 0.0)
        gc_tok = pdot(tri, g_tok)
        gc_row = pdot(g_row, tri_t)
        eg_tok = jnp.exp(gc_tok)
        yc = rows(y, c)
        zc = rows(z_all, c)
        for h in range(heads):
            gcol = gc_tok[:, heads + h:heads + h + 1]
            grow = gc_row[heads + h:heads + h + 1, :]
            beta = beta_tok[:, h:h + 1]
            eg = eg_tok[:, heads + h:heads + h + 1]
            decay = jnp.where(incl, jnp.exp(jnp.where(incl, gcol - grow, 0.0)), 0.0)
            q = yc[:, h * GDN_DK:(h + 1) * GDN_DK]
            k = yc[:, kw + h * GDN_DK:kw + (h + 1) * GDN_DK]
            v = yc[:, 2 * kw + h * GDN_DV:2 * kw + (h + 1) * GDN_DV]
            q = q * lax.rsqrt(jnp.sum(q * q, axis=-1, keepdims=True) + EPS) * (GDN_DK ** -0.5)
            k = k * lax.rsqrt(jnp.sum(k * k, axis=-1, keepdims=True) + EPS)
            kb = k * beta
            m = jnp.where(strict, _dot_nt(kb, k, precise) * decay, 0.0)
            inv = _unit_lower_inverse(m, pdot)
            u = pdot(inv, v * beta)
            w = pdot(inv, kb * eg)
            qk = jnp.where(incl, _dot_nt(q, k, precise) * decay, 0.0)
            g_last = gcol[chunk - 1:chunk, :]
            q_dec = q * eg
            k_dec = k * jnp.exp(g_last - gcol)
            state = s_ref[h]
            v_new = u - _dot(w, state, precise)
            o = _dot(q_dec, state, precise) + _dot(qk, v_new, precise)
            s_ref[h] = state * jnp.exp(g_last) + _dot_tn(k_dec, v_new, precise)
            o = o * lax.rsqrt(jnp.mean(o * o, axis=-1, keepdims=True) + EPS) * gnorm_ref[...]
            zz = zc[:, h * GDN_DV:(h + 1) * GDN_DV]
            out = (o * (zz * _sigmoid(zz))).astype(o_ref.dtype)
            if tt < chunk:
                o_ref[:, h * GDN_DV:(h + 1) * GDN_DV] = out[:tt]
            else:
                o_ref[c * chunk:(c + 1) * chunk, h * GDN_DV:(h + 1) * GDN_DV] = out


def _gdn(proj, ba, prev, s0, w_conv, a_log, dt_bias, gdn_norm, batch, seq, heads, qkv_block, z_block,
         n_valid, precise):
    kw = heads * GDN_DK
    vw = heads * GDN_DV
    cw = 2 * kw + vw
    chunk = GDN_CHUNK
    tt = _tile(seq, 128)
    assert tt % chunk == 0 or tt < chunk
    nt = seq // tt
    bat = jnp.swapaxes(ba[:, :2 * SUBLANES].reshape(batch, seq, 2 * SUBLANES), 1, 2)
    if seq < chunk:
        bat = jnp.pad(bat, ((0, 0), (0, 0), (0, chunk - seq)))
    arow = jnp.zeros((1, LANES), F32).at[0, heads:2 * heads].set(a_log.astype(F32))
    dtrow = jnp.zeros((1, LANES), F32).at[0, heads:2 * heads].set(dt_bias.astype(F32))
    acol = arow[0, :2 * SUBLANES].reshape(2 * SUBLANES, 1)
    dtcol = dtrow[0, :2 * SUBLANES].reshape(2 * SUBLANES, 1)
    const = lambda b, t: (0, 0)
    o, s_new = pl.pallas_call(
        functools.partial(_gdn_body, heads=heads, chunk=chunk, n_valid=n_valid, precise=precise),
        out_shape=(jax.ShapeDtypeStruct((batch * seq, vw), BF16),
                   jax.ShapeDtypeStruct((batch, heads, GDN_DK, GDN_DV), F32)),
        grid=(batch, nt),
        in_specs=[
            pl.BlockSpec((tt, cw), lambda b, t: (b * nt + t, qkv_block)),
            pl.BlockSpec((tt, vw), lambda b, t: (b * nt + t, z_block)),
            pl.BlockSpec((tt, LANES), lambda b, t: (b * nt + t, 0)),
            pl.BlockSpec((None, 2 * SUBLANES, max(tt, chunk)), lambda b, t: (b, 0, t)),
            pl.BlockSpec((None, SUBLANES, cw), lambda b, t: (b, 0, 0)),
            pl.BlockSpec((None, heads, GDN_DK, GDN_DV), lambda b, t: (b, 0, 0, 0)),
            pl.BlockSpec((CONV_W, cw), const),
            pl.BlockSpec((1, LANES), const),
            pl.BlockSpec((1, LANES), const),
            pl.BlockSpec((2 * SUBLANES, 1), const),
            pl.BlockSpec((2 * SUBLANES, 1), const),
            pl.BlockSpec((1, GDN_DV), const),
        ],
        out_specs=(pl.BlockSpec((tt, vw), lambda b, t: (b * nt + t, 0)),
                   pl.BlockSpec((None, heads, GDN_DK, GDN_DV), lambda b, t: (b, 0, 0, 0))),
        scratch_shapes=[pltpu.VMEM((SUBLANES, cw), F32)],
        compiler_params=_params("parallel", "arbitrary"),
        name="gdn",
    )(proj, proj, ba, bat, prev, s0, w_conv.astype(F32), arow, dtrow, acol, dtcol,
      gdn_norm.reshape(1, GDN_DV).astype(F32))
    return o, s_new


def _mem_attn_body(q_ref, k_ref, v_ref, g_ref, o_ref, *, heads, dh):
    scale = dh ** -0.5
    for h in range(heads):
        hs = slice(h * dh, (h + 1) * dh)
        q = q_ref[:, hs]
        q = q * lax.rsqrt(jnp.mean(q * q, axis=-1, keepdims=True) + EPS) * g_ref[...]
        s = _dot_nt(q, k_ref[:, hs]) * scale
        e = jnp.exp(s - jnp.max(s, axis=-1, keepdims=True))
        p = e / jnp.sum(e, axis=-1, keepdims=True)
        o_ref[:, hs] = _dot(p, v_ref[:, hs]).astype(o_ref.dtype)


def _mem_attn(proj, mem_k, mem_v, q_norm, batch, seq, q_block):
    n_mem, heads, dh = mem_k.shape[1:]
    w = heads * dh
    tt = _tile(seq, 512)
    nt = seq // tt
    return pl.pallas_call(
        functools.partial(_mem_attn_body, heads=heads, dh=dh),
        out_shape=jax.ShapeDtypeStruct((batch * seq, w), BF16),
        grid=(batch, nt),
        in_specs=[
            pl.BlockSpec((tt, w), lambda b, t: (b * nt + t, q_block)),
            pl.BlockSpec((None, n_mem, w), lambda b, t: (b, 0, 0)),
            pl.BlockSpec((None, n_mem, w), lambda b, t: (b, 0, 0)),
            pl.BlockSpec((1, dh), lambda b, t: (0, 0)),
        ],
        out_specs=pl.BlockSpec((tt, w), lambda b, t: (b * nt + t, 0)),
        compiler_params=_params("parallel", "parallel"),
        name="mem_attn",
    )(proj, mem_k.reshape(batch, n_mem, w), mem_v.reshape(batch, n_mem, w), q_norm.reshape(1, dh).astype(F32))


def _merge_body(a0_ref, a1_ref, a2_ref, w0_ref, w1_ref, w2_ref, g0_ref, g1_ref, g2_ref, o_ref):
    acc = _sigmoid(g0_ref[...]) * jnp.dot(a0_ref[...], w0_ref[...], preferred_element_type=F32)
    acc += _sigmoid(g1_ref[...]) * jnp.dot(a1_ref[...], w1_ref[...], preferred_element_type=F32)
    acc += _sigmoid(g2_ref[...]) * jnp.dot(a2_ref[...], w2_ref[...], preferred_element_type=F32)
    o_ref[...] = acc.astype(o_ref.dtype)


def _merge(branches, weights, proj, gate_col, d_model):
    m = proj.shape[0]
    tm = _tile(m, 512)
    tn = _tile(d_model, 512)
    a_specs = [pl.BlockSpec((tm, a.shape[1]), lambda i, j: (i, 0)) for a in branches]
    w_specs = [pl.BlockSpec((w.shape[0], tn), lambda i, j: (0, j)) for w in weights]
    g_specs = [pl.BlockSpec((tm, tn), functools.partial(lambda i, j, off: (i, off + j), off=(gate_col + g * d_model) // tn))
               for g in range(3)]
    return pl.pallas_call(
        _merge_body,
        out_shape=jax.ShapeDtypeStruct((m, d_model), BF16),
        grid=(m // tm, d_model // tn),
        in_specs=a_specs + w_specs + g_specs,
        out_specs=pl.BlockSpec((tm, tn), lambda i, j: (i, j)),
        compiler_params=_params("parallel", "arbitrary"),
        name="merge",
    )(*branches, *weights, proj, proj, proj)


def _ffn_body(h_ref, wg_ref, wu_ref, wo_ref, x_ref, o_ref):
    f = pl.program_id(1)
    h = h_ref[...]
    gate = jnp.dot(h, wg_ref[...], preferred_element_type=F32)
    up = jnp.dot(h, wu_ref[...], preferred_element_type=F32)
    act = (gate * _sigmoid(gate) * up).astype(BF16)
    part = jnp.dot(act, wo_ref[...], preferred_element_type=F32)

    @pl.when(f == 0)
    def _():
        o_ref[...] = x_ref[...] + part

    @pl.when(f > 0)
    def _():
        o_ref[...] += part


def _ffn(h, x, w_in, w_out):
    m, d = x.shape
    dff = w_out.shape[0]
    tm = _tile(m, 512)
    tf = _tile(dff, 512)
    nf = dff // tf
    return pl.pallas_call(
        _ffn_body,
        out_shape=jax.ShapeDtypeStruct((m, d), F32),
        grid=(m // tm, nf),
        in_specs=[
            pl.BlockSpec((tm, d), lambda i, f: (i, 0)),
            pl.BlockSpec((d, tf), lambda i, f: (0, f)),
            pl.BlockSpec((d, tf), lambda i, f: (0, nf + f)),
            pl.BlockSpec((tf, d), lambda i, f: (f, 0)),
            pl.BlockSpec((tm, d), lambda i, f: (i, 0)),
        ],
        out_specs=pl.BlockSpec((tm, d), lambda i, f: (i, 0)),
        compiler_params=_params("parallel", "arbitrary"),
        name="ffn",
    )(h, w_in, w_in, w_out, x)


def _trunk(x, p, sb_attend, conv_prev, s0, mem_k, mem_v, n_valid, precise_gdn):
    batch, seq, d = x.shape
    heads = p["sb_heads"]
    x2 = x.reshape(batch * seq, d)
    h = _rmsnorm(x2, p["ln_mix"])
    proj = _matmul(h, p["w_main"])
    ba = _matmul(h, p["w_ba"])
    o_sb = sb_attend(proj)
    o_gdn, s_new = _gdn(proj, ba, conv_prev, s0, p["w_conv"], p["a_log"], p["dt_bias"], p["gdn_norm"],
                        batch, seq, p["gdn_heads"], qkv_block=1, z_block=p["z_block"],
                        n_valid=n_valid, precise=precise_gdn)
    o_mem = _mem_attn(proj, mem_k, mem_v, p["q_norm_mem"], batch, seq, q_block=p["memq_block"])
    merged = _merge([o_sb, o_gdn, o_mem], [p["w_proj_sb"], p["w_proj_gdn"], p["w_proj_mem"]],
                    proj, p["gate_col"], d)
    x1 = _matmul(merged, p["w_out"], res=x2)
    y = _ffn(_rmsnorm(x1, p["ln_ffn"]), x1, p["w_ffn_in"], p["w_ffn_out"])
    return y.reshape(batch, seq, d), proj, s_new


def kernel(x_prompt, x_sample, mem_prompt, cache_sb_k, cache_sb_v, state_gdn, state_conv, cache_mem_k, cache_mem_v, page_table, ln_mix, w_in, sb_bias, w_conv, a_log, dt_bias, gdn_norm, q_norm_mem, k_norm_mem, ln_mem, w_mem_kv, w_proj_sb, w_proj_gdn, w_proj_mem, w_out, ln_ffn, w_ffn_in, w_ffn_out):
    depth = w_in.shape[0]
    d_model = x_prompt.shape[-1]
    bp, tp = x_prompt.shape[:2]
    bs, ts = x_sample.shape[:2]
    sb_heads = sb_bias.shape[-1]
    gdn_heads = a_log.shape[-1]
    mem_heads, mem_dim = cache_mem_k.shape[-2:]
    n_mem = mem_prompt.shape[1]
    sb_w = sb_heads * SB_DIM
    kw = gdn_heads * GDN_DK
    vw = gdn_heads * GDN_DV
    cw = 2 * kw + vw
    mem_w = mem_heads * mem_dim
    assert sb_w == kw == vw == mem_w and cw == 3 * sb_w and 2 * gdn_heads <= 2 * SUBLANES
    off_ba = 3 * sb_w + cw + vw
    off_memq = off_ba + 2 * gdn_heads
    main_w = off_ba + mem_w + 3 * d_model
    assert w_in.shape[-1] == off_memq + mem_w + 3 * d_model

    yp = x_prompt
    ys = jnp.pad(x_sample, ((0, 0), (0, SAMPLE_PAD - ts), (0, 0)))
    outs = [[] for _ in range(10)]
    for l in range(depth):
        w_main = jnp.concatenate([w_in[l][:, :off_ba], w_in[l][:, off_memq:]], axis=1).astype(BF16)
        w_ba = jnp.pad(w_in[l][:, off_ba:off_memq], ((0, 0), (0, LANES - 2 * gdn_heads))).astype(BF16)
        p = {
            "sb_heads": sb_heads, "gdn_heads": gdn_heads,
            "z_block": (3 * sb_w + cw) // vw, "memq_block": off_ba // mem_w, "gate_col": off_ba + mem_w,
            "ln_mix": ln_mix[l], "w_main": w_main, "w_ba": w_ba, "w_conv": w_conv[l], "a_log": a_log[l],
            "dt_bias": dt_bias[l], "gdn_norm": gdn_norm[l], "q_norm_mem": q_norm_mem[l],
            "w_proj_sb": w_proj_sb[l].astype(BF16), "w_proj_gdn": w_proj_gdn[l].astype(BF16),
            "w_proj_mem": w_proj_mem[l].astype(BF16), "w_out": w_out[l].astype(BF16), "ln_ffn": ln_ffn[l],
            "w_ffn_in": w_ffn_in[l].astype(BF16), "w_ffn_out": w_ffn_out[l].astype(BF16),
        }
        assert main_w == w_main.shape[1]

        kv = _matmul(_rmsnorm(mem_prompt.reshape(bp * n_mem, d_model), ln_mem[l]), w_mem_kv[l].astype(BF16))
        mk = _headnorm(kv, k_norm_mem[l], mem_heads, 0).reshape(bp, n_mem, mem_heads, mem_dim)
        mv = kv[:, mem_w:].reshape(bp, n_mem, mem_heads, mem_dim)
        yp, proj_p, sp = _trunk(
            yp, p, lambda proj: _sb_prompt(proj, sb_bias[l], bp, tp, sb_heads),
            jnp.zeros((bp, SUBLANES, cw), F32), jnp.zeros((bp, gdn_heads, GDN_DK, GDN_DV), F32),
            mk, mv, n_valid=None, precise_gdn=False)

        conv_prev = jnp.pad(state_conv[l], ((0, 0), (SUBLANES - (CONV_W - 1), 0), (0, 0)))
        ys, proj_s, ss = _trunk(
            ys, p, lambda proj: _sb_sample(proj, sb_bias[l], cache_sb_k, cache_sb_v, page_table, l, sb_heads),
            conv_prev, state_gdn[l], cache_mem_k[l], cache_mem_v[l], n_valid=ts, precise_gdn=True)

        pp = proj_p.reshape(bp, tp, main_w)
        ps = proj_s.reshape(bs, SAMPLE_PAD, main_w)
        outs[0].append(pp[:, :, sb_w:2 * sb_w].reshape(bp, tp, sb_heads, SB_DIM))
        outs[1].append(pp[:, :, 2 * sb_w:3 * sb_w].reshape(bp, tp, sb_heads, SB_DIM))
        outs[2].append(ps[:, :ts, sb_w:2 * sb_w].reshape(bs, ts, sb_heads, SB_DIM))
        outs[3].append(ps[:, :ts, 2 * sb_w:3 * sb_w].reshape(bs, ts, sb_heads, SB_DIM))
        outs[4].append(sp)
        outs[5].append(ss)
        outs[6].append(pp[:, tp - (CONV_W - 1):, 3 * sb_w:3 * sb_w + cw])
        outs[7].append(ps[:, ts - (CONV_W - 1):ts, 3 * sb_w:3 * sb_w + cw])
        outs[8].append(mk)
        outs[9].append(mv)
    return (yp, ys[:, :ts]) + tuple(jnp.stack(o) for o in outs)
```

```python
import functools

import jax
import jax.numpy as jnp
from jax import lax
from jax.experimental import pallas as pl
from jax.experimental.pallas import tpu as pltpu

F32 = jnp.float32
BF16 = jnp.bfloat16

EPS = 1e-6
LOG2E = 1.4426950408889634
SB_DIM = 128
SB_RUN = 256
GDN_DK = 128
GDN_DV = 128
GDN_CHUNK = 64
CONV_W = 4
SUBLANES = 8
LANES = 128
SAMPLE_PAD = 2 * SUBLANES
VMEM_LIMIT = 56 * 1024 * 1024


def _params(*sem):
    return pltpu.CompilerParams(dimension_semantics=sem, vmem_limit_bytes=VMEM_LIMIT)


def _tile(n, pref):
    if n <= pref:
        return n
    t = pref
    while n % t:
        t //= 2
    return t


def _dot(a, b):
    return jnp.dot(a.astype(BF16), b.astype(BF16), preferred_element_type=F32)


def _dot_nt(a, b):
    return lax.dot_general(a.astype(BF16), b.astype(BF16), (((1,), (1,)), ((), ())), preferred_element_type=F32)


def _sigmoid(x):
    return 1.0 / (1.0 + jnp.exp(-x))


def _softplus(x):
    return jnp.maximum(x, 0.0) + jnp.log(1.0 + jnp.exp(-jnp.abs(x)))


def _rmsnorm_body(x_ref, g_ref, o_ref):
    x = x_ref[...].astype(F32)
    ms = jnp.mean(x * x, axis=-1, keepdims=True)
    o_ref[...] = (x * lax.rsqrt(ms + EPS) * g_ref[...]).astype(o_ref.dtype)


def _rmsnorm(x, g, out_dtype=BF16):
    m, d = x.shape
    tm = _tile(m, 512)
    return pl.pallas_call(
        _rmsnorm_body,
        out_shape=jax.ShapeDtypeStruct((m, d), out_dtype),
        grid=(m // tm,),
        in_specs=[pl.BlockSpec((tm, d), lambda i: (i, 0)), pl.BlockSpec((1, d), lambda i: (0, 0))],
        out_specs=pl.BlockSpec((tm, d), lambda i: (i, 0)),
        compiler_params=_params("parallel"),
        name="rmsnorm",
    )(x, g.reshape(1, d).astype(F32))


def _headnorm_body(x_ref, g_ref, o_ref, *, heads, dh):
    for h in range(heads):
        x = x_ref[:, h * dh:(h + 1) * dh].astype(F32)
        ms = jnp.mean(x * x, axis=-1, keepdims=True)
        o_ref[:, h * dh:(h + 1) * dh] = (x * lax.rsqrt(ms + EPS) * g_ref[...]).astype(o_ref.dtype)


def _headnorm(x, g, heads, col_block):
    m = x.shape[0]
    dh = g.shape[-1]
    w = heads * dh
    tm = _tile(m, 512)
    return pl.pallas_call(
        functools.partial(_headnorm_body, heads=heads, dh=dh),
        out_shape=jax.ShapeDtypeStruct((m, w), F32),
        grid=(m // tm,),
        in_specs=[pl.BlockSpec((tm, w), lambda i: (i, col_block)), pl.BlockSpec((1, dh), lambda i: (0, 0))],
        out_specs=pl.BlockSpec((tm, w), lambda i: (i, 0)),
        compiler_params=_params("parallel"),
        name="headnorm",
    )(x, g.reshape(1, dh).astype(F32))


def _mm_body(a_ref, b_ref, o_ref):
    o_ref[...] = jnp.dot(a_ref[...], b_ref[...], preferred_element_type=F32).astype(o_ref.dtype)


def _mm_res_body(a_ref, b_ref, r_ref, o_ref):
    o_ref[...] = (r_ref[...] + jnp.dot(a_ref[...], b_ref[...], preferred_element_type=F32)).astype(o_ref.dtype)


def _matmul(a, b, res=None, out_dtype=F32):
    m, k = a.shape
    n = b.shape[1]
    tm = _tile(m, 1024)
    tn = _tile(n, 512)
    in_specs = [pl.BlockSpec((tm, k), lambda i, j: (i, 0)), pl.BlockSpec((k, tn), lambda i, j: (0, j))]
    args = [a, b]
    body = _mm_body
    if res is not None:
        in_specs.append(pl.BlockSpec((tm, tn), lambda i, j: (i, j)))
        args.append(res)
        body = _mm_res_body
    return pl.pallas_call(
        body,
        out_shape=jax.ShapeDtypeStruct((m, n), out_dtype),
        grid=(m // tm, n // tn),
        in_specs=in_specs,
        out_specs=pl.BlockSpec((tm, tn), lambda i, j: (i, j)),
        compiler_params=_params("parallel", "arbitrary"),
        name="matmul",
    )(*args)


def _sb_blocks(items, later):
    tk = later.shape[0]
    lanes = LANES
    zs = [_dot_nt(q, k) + bias for (q, k, _, _, bias, _) in items]
    staged = []
    for (q, k, _, _, _, mask), z in zip(items, zs):
        nb = k.shape[0] // tk
        neg_abs = lax.bitcast_convert_type(lax.bitcast_convert_type(z, jnp.uint32) | jnp.uint32(0x80000000), F32)
        sp = jnp.maximum(z, 0.0) + jnp.log2(1.0 + jnp.exp2(neg_abs))
        log_hit = z - sp
        if mask is not None:
            sp = jnp.where(mask, sp, 0.0)
        hi = lax.bitcast_convert_type(lax.bitcast_convert_type(sp, jnp.uint32) & jnp.uint32(0xFFFF0000), F32)
        lo = sp - hi
        parts = [hi[:, b * tk:(b + 1) * tk] for b in range(nb)] + [lo[:, b * tk:(b + 1) * tk] for b in range(nb)]
        staged.append((log_hit, sp, jnp.concatenate(parts, axis=0).astype(BF16)))
    sums = [jnp.dot(cat, later, preferred_element_type=F32) for (_, _, cat) in staged]
    weights = []
    for (q, k, _, carry, _, mask), (log_hit, sp, _), s in zip(items, staged, sums):
        r = q.shape[0]
        nb = k.shape[0] // tk
        pre = [None] * nb
        for b in reversed(range(nb)):
            after = s[b * r:(b + 1) * r] + s[(nb + b) * r:(nb + b + 1) * r]
            pre[b] = log_hit[:, b * tk:(b + 1) * tk] - after - jnp.concatenate([carry] * (tk // lanes), axis=1)
            total = after[:, 0:1] + sp[:, b * tk:b * tk + 1]
            carry = carry + jnp.broadcast_to(total, (r, lanes))
        a = jnp.exp2(jnp.concatenate(pre, axis=1) if nb > 1 else pre[0])
        if mask is not None:
            a = jnp.where(mask, a, 0.0)
        weights.append((a.astype(BF16), carry))
    return [(jnp.dot(a, v, preferred_element_type=F32), carry) for (a, carry), (_, _, v, _, _, _) in zip(weights, items)]


def _later_key_indicator(tk):
    j = lax.broadcasted_iota(jnp.int32, (tk, tk), 0)
    s = lax.broadcasted_iota(jnp.int32, (tk, tk), 1)
    return jnp.where(j > s, 1.0, 0.0).astype(BF16)


def _sb_prompt_body(bias_ref, q_ref, k_ref, v_ref, later_ref, o_ref, kb_ref, vb_ref, qs_ref, acc_ref, carry_ref,
                    *, tq, rb, kb, scale):
    h = pl.program_id(1)
    qi = pl.program_id(2)
    run = later_ref.shape[0]

    @pl.when(qi == 0)
    def _():
        kb_ref[...] = k_ref[...].astype(BF16)
        vb_ref[...] = v_ref[...].astype(BF16)

    qs_ref[...] = (q_ref[...] * (scale * LOG2E)).astype(BF16)
    acc_ref[...] = jnp.zeros_like(acc_ref)
    carry_ref[...] = jnp.zeros_like(carry_ref)
    bias = bias_ref[h] * LOG2E
    later = later_ref[...]

    def update(start, width, mask):
        start = pl.multiple_of(start, run)
        k = kb_ref[pl.ds(start, width), :]
        v = vb_ref[pl.ds(start, width), :]
        groups = [(r0, r0 + rb) for r0 in range(0, tq, rb)]
        items = [(qs_ref[r0:r1], k, v, carry_ref[r0:r1], bias, None if mask is None else mask[r0:r1])
                 for r0, r1 in groups]
        for (r0, r1), (contrib, carry) in zip(groups, _sb_blocks(items, later)):
            acc_ref[r0:r1] += contrib
            carry_ref[r0:r1] = carry

    causal = lax.broadcasted_iota(jnp.int32, (tq, tq), 1) < lax.broadcasted_iota(jnp.int32, (tq, tq), 0)
    update(qi * tq, tq, causal)

    width = kb * run
    n_runs = (qi * tq) // width

    def body(jj, c):
        update((n_runs - 1 - jj) * width, width, None)
        return c

    lax.fori_loop(0, n_runs, body, 0)
    o_ref[...] = acc_ref[...].astype(o_ref.dtype)


def _sb_prompt(proj, bias, batch, seq, heads, tq=512, rb=256, kb=2):
    d = SB_DIM
    run = SB_RUN
    tq = _tile(seq, tq)
    rb = min(rb, tq)
    kb = min(kb, tq // run)
    assert tq % (kb * run) == 0 and tq % rb == 0
    nq = seq // tq
    return pl.pallas_call(
        functools.partial(_sb_prompt_body, tq=tq, rb=rb, kb=kb, scale=d ** -0.5),
        out_shape=jax.ShapeDtypeStruct((batch * seq, heads * d), BF16),
        grid=(batch, heads, nq),
        in_specs=[
            pl.BlockSpec(memory_space=pltpu.SMEM),
            pl.BlockSpec((tq, d), lambda b, h, i: (b * nq + i, h)),
            pl.BlockSpec((seq, d), lambda b, h, i: (b, heads + h)),
            pl.BlockSpec((seq, d), lambda b, h, i: (b, 2 * heads + h)),
            pl.BlockSpec((run, run), lambda b, h, i: (0, 0)),
        ],
        out_specs=pl.BlockSpec((tq, d), lambda b, h, i: (b * nq + i, h)),
        scratch_shapes=[pltpu.VMEM((seq, d), BF16), pltpu.VMEM((seq, d), BF16), pltpu.VMEM((tq, d), BF16),
                        pltpu.VMEM((tq, d), F32), pltpu.VMEM((tq, LANES), F32)],
        compiler_params=_params("parallel", "parallel", "arbitrary"),
        name="sb_prompt",
    )(bias.astype(F32), proj, proj, proj, _later_key_indicator(run))


def _sb_sample_body(pt_ref, bias_ref, q_ref, kn_ref, vn_ref, later_ref, *refs, heads, pages_per_step, page, scale):
    del pt_ref
    page_refs = refs[:2 * pages_per_step]
    o_ref, acc_ref, carry_ref = refs[2 * pages_per_step:]
    s = pl.program_id(1)
    d = SB_DIM
    run = later_ref.shape[0]
    qp = q_ref.shape[0]
    q = (q_ref[...] * (scale * LOG2E)).astype(BF16)
    later = later_ref[...]
    cols = [slice(h * d, (h + 1) * d) for h in range(heads)]

    @pl.when(s == 0)
    def _():
        pad = jnp.zeros((run - qp, heads * d), F32)
        k_new = jnp.concatenate([kn_ref[...], pad], axis=0).astype(BF16)
        v_new = jnp.concatenate([vn_ref[...], pad], axis=0).astype(BF16)
        mask = lax.broadcasted_iota(jnp.int32, (qp, run), 1) < lax.broadcasted_iota(jnp.int32, (qp, run), 0)
        items = [(q[:, c], k_new[:, c], v_new[:, c], jnp.zeros((qp, LANES), F32), bias_ref[h] * LOG2E, mask)
                 for h, c in enumerate(cols)]
        for c, (contrib, carry) in zip(cols, _sb_blocks(items, later)):
            acc_ref[:, c] = contrib
            carry_ref[:, c] = carry

    def head_rows(ref, h):
        return ref[pl.ds(h, page, stride=heads), :].astype(BF16)

    order = list(reversed(range(pages_per_step)))
    items = []
    for h, c in enumerate(cols):
        k = jnp.concatenate([head_rows(page_refs[2 * i], h) for i in order], axis=0)
        v = jnp.concatenate([head_rows(page_refs[2 * i + 1], h) for i in order], axis=0)
        items.append((q[:, c], k, v, carry_ref[:, c], bias_ref[h] * LOG2E, None))
    for c, (contrib, carry) in zip(cols, _sb_blocks(items, later)):
        acc_ref[:, c] += contrib
        carry_ref[:, c] = carry

    @pl.when(s == pl.num_programs(1) - 1)
    def _():
        o_ref[...] = acc_ref[...].astype(o_ref.dtype)


def _sb_sample(proj, bias, cache_k, cache_v, page_table, layer, heads):
    d = SB_DIM
    run = SB_RUN
    batch, n_pages = page_table.shape
    depth, n_phys, page = cache_k.shape[:3]
    w = heads * d
    qp = SAMPLE_PAD
    pps = _tile(n_pages, 8)
    assert (pps * page) % run == 0
    steps = n_pages // pps
    ck = cache_k.reshape(depth * n_phys, page * heads, d)
    cv = cache_v.reshape(depth * n_phys, page * heads, d)

    def page_spec(i):
        def index(b, s, pt):
            return (layer * n_phys + pt[b * n_pages + (n_pages - 1 - (s * pps + i))], 0, 0)
        return pl.BlockSpec((None, page * heads, d), index)

    page_specs = []
    page_args = []
    for i in range(pps):
        page_specs += [page_spec(i), page_spec(i)]
        page_args += [ck, cv]
    grid_spec = pltpu.PrefetchScalarGridSpec(
        num_scalar_prefetch=1,
        grid=(batch, steps),
        in_specs=[
            pl.BlockSpec(memory_space=pltpu.SMEM),
            pl.BlockSpec((qp, w), lambda b, s, pt: (b, 0)),
            pl.BlockSpec((qp, w), lambda b, s, pt: (b, 1)),
            pl.BlockSpec((qp, w), lambda b, s, pt: (b, 2)),
            pl.BlockSpec((run, run), lambda b, s, pt: (0, 0)),
        ] + page_specs,
        out_specs=pl.BlockSpec((qp, w), lambda b, s, pt: (b, 0)),
        scratch_shapes=[pltpu.VMEM((qp, w), F32), pltpu.VMEM((qp, w), F32)],
    )
    return pl.pallas_call(
        functools.partial(_sb_sample_body, heads=heads, pages_per_step=pps, page=page, scale=d ** -0.5),
        out_shape=jax.ShapeDtypeStruct((batch * qp, w), BF16),
        grid_spec=grid_spec,
        compiler_params=_params("parallel", "arbitrary"),
        name="sb_sample",
    )(page_table.reshape(-1).astype(jnp.int32), bias.astype(F32), proj, proj, proj,
      _later_key_indicator(run), *page_args)


_NN = (((1,), (0,)), ((), ()))
_NT = (((1,), (1,)), ((), ()))
_TN = (((0,), (0,)), ((), ()))


def _split_bf16(x):
    hi = x.astype(BF16)
    return hi, (x - hi.astype(F32)).astype(BF16)


def _mm(a, b, dims, precise):
    dot = lambda x, y: lax.dot_general(x, y, dims, preferred_element_type=F32)
    if not precise:
        return dot(a.astype(BF16), b.astype(BF16))
    ah, al = _split_bf16(a)
    bh, bl = _split_bf16(b)
    return dot(ah, bh) + dot(ah, bl) + dot(al, bh)


def _split3_bf16(x):
    hi = x.astype(BF16)
    rest = x - hi.astype(F32)
    mid = rest.astype(BF16)
    return hi, mid, (rest - mid.astype(F32)).astype(BF16)


def _mm_exact_lhs(a, b):
    a = a.astype(BF16)
    return sum(jnp.dot(a, piece, preferred_element_type=F32) for piece in _split3_bf16(b))


def _mm_exact_rhs(a, b):
    b = b.astype(BF16)
    return sum(jnp.dot(piece, b, preferred_element_type=F32) for piece in _split3_bf16(a))


def _unit_lower_inverses(ms):
    c = ms[0].shape[0]
    eye = (lax.broadcasted_iota(jnp.int32, (c, c), 0) == lax.broadcasted_iota(jnp.int32, (c, c), 1)).astype(F32)
    ps = [-m for m in ms]
    invs = [eye + p for p in ps]
    span = 2
    while span < c:
        ps = [_mm(p, p, _NN, True) for p in ps]
        invs = [inv + _mm(inv, p, _NN, True) for inv, p in zip(invs, ps)]
        span *= 2
    return invs


def _gdn_body(qkv_ref, z_ref, ba_ref, bat_ref, prev_ref, s0_ref, wconv_ref, arow_ref, dtrow_ref,
              acol_ref, dtcol_ref, gnorm_ref, o_ref, s_ref, tail_ref, *, heads, chunk, n_valid, precise):
    t = pl.program_id(1)
    tt = qkv_ref.shape[0]
    kw = heads * GDN_DK
    sub = SUBLANES

    @pl.when(t == 0)
    def _():
        tail_ref[...] = prev_ref[...]
        s_ref[...] = s0_ref[...]

    x = qkv_ref[...]
    head_rows = jnp.concatenate([tail_ref[...], x[:sub]], axis=0)
    y = x * wconv_ref[CONV_W - 1:CONV_W, :]
    for shift in range(1, CONV_W):
        first = pltpu.roll(head_rows, shift, 0)[sub:]
        if tt > sub:
            shifted = jnp.concatenate([first, pltpu.roll(x, shift, 0)[sub:]], axis=0)
        else:
            shifted = first
        y = y + shifted * wconv_ref[CONV_W - 1 - shift:CONV_W - shift, :]
    tail_ref[...] = x[tt - sub:]
    y = y * _sigmoid(y)

    n_chunks = max(tt, chunk) // chunk

    def rows(a, c):
        if tt < chunk:
            return jnp.concatenate([a, jnp.zeros((chunk - tt, a.shape[1]), a.dtype)], axis=0)
        return a[c * chunk:(c + 1) * chunk]

    ri = lax.broadcasted_iota(jnp.int32, (chunk, chunk), 0)
    ci = lax.broadcasted_iota(jnp.int32, (chunk, chunk), 1)
    incl = ri >= ci
    strict = ri > ci
    tri = incl.astype(F32)
    tri_t = (ri <= ci).astype(F32)

    items = []
    for c in range(n_chunks):
        ba = rows(ba_ref[...], c)
        g_tok = -jnp.exp(arow_ref[...]) * _softplus(ba + dtrow_ref[...])
        beta_tok = _sigmoid(ba)
        bat = bat_ref[:, c * chunk:(c + 1) * chunk]
        g_row = -jnp.exp(acol_ref[...]) * _softplus(bat + dtcol_ref[...])
        if n_valid is not None:
            valid_r = lax.broadcasted_iota(jnp.int32, g_tok.shape, 0) < n_valid
            g_tok = jnp.where(valid_r, g_tok, 0.0)
            beta_tok = jnp.where(valid_r, beta_tok, 0.0)
            g_row = jnp.where(lax.broadcasted_iota(jnp.int32, g_row.shape, 1) < n_valid, g_row, 0.0)
        gc_tok = _mm_exact_lhs(tri, g_tok)
        gc_row = _mm_exact_rhs(g_row, tri_t)
        eg_tok = jnp.exp(gc_tok)
        yc = rows(y, c)
        for h in range(heads):
            gcol = gc_tok[:, heads + h:heads + h + 1]
            grow = gc_row[heads + h:heads + h + 1, :]
            beta = beta_tok[:, h:h + 1]
            eg = eg_tok[:, heads + h:heads + h + 1]
            q = yc[:, h * GDN_DK:(h + 1) * GDN_DK]
            k = yc[:, kw + h * GDN_DK:kw + (h + 1) * GDN_DK]
            v = yc[:, 2 * kw + h * GDN_DV:2 * kw + (h + 1) * GDN_DV]
            q = q * lax.rsqrt(jnp.sum(q * q, axis=-1, keepdims=True) + EPS) * (GDN_DK ** -0.5)
            k = k * lax.rsqrt(jnp.sum(k * k, axis=-1, keepdims=True) + EPS)
            g_last = gcol[chunk - 1:chunk, :]
            items.append(dict(
                c=c, h=h, q=q, k=k, kb=k * beta, g_last=g_last,
                decay=jnp.where(incl, jnp.exp(jnp.where(incl, gcol - grow, 0.0)), 0.0),
                rhs=jnp.concatenate([v * beta, (k * beta) * eg], axis=1),
                q_dec=q * eg, k_dec=k * jnp.exp(g_last - gcol)))
    kks = [_mm(it["kb"], it["k"], _NT, precise) for it in items]
    qks = [_mm(it["q"], it["k"], _NT, precise) for it in items]
    invs = _unit_lower_inverses([jnp.where(strict, kk * it["decay"], 0.0) for kk, it in zip(kks, items)])
    uws = [_mm(inv, it["rhs"], _NN, True) for inv, it in zip(invs, items)]
    qks = [jnp.where(incl, qk * it["decay"], 0.0) for qk, it in zip(qks, items)]

    for c in range(n_chunks):
        sel = [i for i, it in enumerate(items) if it["c"] == c]
        states = [s_ref[items[i]["h"]] for i in sel]
        ws = [_mm(jnp.concatenate([uws[i][:, GDN_DV:], items[i]["q_dec"]], axis=0), s, _NN, precise)
              for i, s in zip(sel, states)]
        v_news = [uws[i][:, :GDN_DV] - w[:chunk] for i, w in zip(sel, ws)]
        outs = [w[chunk:] + _mm(qks[i], vn, _NN, precise) for i, w, vn in zip(sel, ws, v_news)]
        deltas = [_mm(items[i]["k_dec"], vn, _TN, precise) for i, vn in zip(sel, v_news)]
        zc = rows(z_ref[...], c)
        for i, s, o, delta in zip(sel, states, outs, deltas):
            h = items[i]["h"]
            s_ref[h] = s * jnp.exp(items[i]["g_last"]) + delta
            o = o * lax.rsqrt(jnp.mean(o * o, axis=-1, keepdims=True) + EPS) * gnorm_ref[...]
            zz = zc[:, h * GDN_DV:(h + 1) * GDN_DV]
            out = (o * (zz * _sigmoid(zz))).astype(o_ref.dtype)
            if tt < chunk:
                o_ref[:, h * GDN_DV:(h + 1) * GDN_DV] = out[:tt]
            else:
                o_ref[c * chunk:(c + 1) * chunk, h * GDN_DV:(h + 1) * GDN_DV] = out


def _gdn(proj, ba, prev, s0, w_conv, a_log, dt_bias, gdn_norm, batch, seq, heads, qkv_block, z_block,
         n_valid, precise):
    kw = heads * GDN_DK
    vw = heads * GDN_DV
    cw = 2 * kw + vw
    chunk = GDN_CHUNK
    tt = _tile(seq, 128)
    assert tt % chunk == 0 or tt < chunk
    nt = seq // tt
    bat = jnp.swapaxes(ba[:, :2 * SUBLANES].reshape(batch, seq, 2 * SUBLANES), 1, 2)
    if seq < chunk:
        bat = jnp.pad(bat, ((0, 0), (0, 0), (0, chunk - seq)))
    arow = jnp.zeros((1, LANES), F32).at[0, heads:2 * heads].set(a_log.astype(F32))
    dtrow = jnp.zeros((1, LANES), F32).at[0, heads:2 * heads].set(dt_bias.astype(F32))
    acol = arow[0, :2 * SUBLANES].reshape(2 * SUBLANES, 1)
    dtcol = dtrow[0, :2 * SUBLANES].reshape(2 * SUBLANES, 1)
    const = lambda b, t: (0, 0)
    o, s_new = pl.pallas_call(
        functools.partial(_gdn_body, heads=heads, chunk=chunk, n_valid=n_valid, precise=precise),
        out_shape=(jax.ShapeDtypeStruct((batch * seq, vw), BF16),
                   jax.ShapeDtypeStruct((batch, heads, GDN_DK, GDN_DV), F32)),
        grid=(batch, nt),
        in_specs=[
            pl.BlockSpec((tt, cw), lambda b, t: (b * nt + t, qkv_block)),
            pl.BlockSpec((tt, vw), lambda b, t: (b * nt + t, z_block)),
            pl.BlockSpec((tt, LANES), lambda b, t: (b * nt + t, 0)),
            pl.BlockSpec((None, 2 * SUBLANES, max(tt, chunk)), lambda b, t: (b, 0, t)),
            pl.BlockSpec((None, SUBLANES, cw), lambda b, t: (b, 0, 0)),
            pl.BlockSpec((None, heads, GDN_DK, GDN_DV), lambda b, t: (b, 0, 0, 0)),
            pl.BlockSpec((CONV_W, cw), const),
            pl.BlockSpec((1, LANES), const),
            pl.BlockSpec((1, LANES), const),
            pl.BlockSpec((2 * SUBLANES, 1), const),
            pl.BlockSpec((2 * SUBLANES, 1), const),
            pl.BlockSpec((1, GDN_DV), const),
        ],
        out_specs=(pl.BlockSpec((tt, vw), lambda b, t: (b * nt + t, 0)),
                   pl.BlockSpec((None, heads, GDN_DK, GDN_DV), lambda b, t: (b, 0, 0, 0))),
        scratch_shapes=[pltpu.VMEM((SUBLANES, cw), F32)],
        compiler_params=_params("parallel", "arbitrary"),
        name="gdn",
    )(proj, proj, ba, bat, prev, s0, w_conv.astype(F32), arow, dtrow, acol, dtcol,
      gdn_norm.reshape(1, GDN_DV).astype(F32))
    return o, s_new


def _mem_attn_body(q_ref, k_ref, v_ref, g_ref, o_ref, *, heads, dh):
    scale = dh ** -0.5
    for h in range(heads):
        hs = slice(h * dh, (h + 1) * dh)
        q = q_ref[:, hs]
        q = q * lax.rsqrt(jnp.mean(q * q, axis=-1, keepdims=True) + EPS) * g_ref[...]
        s = _dot_nt(q, k_ref[:, hs]) * scale
        e = jnp.exp(s - jnp.max(s, axis=-1, keepdims=True))
        p = e / jnp.sum(e, axis=-1, keepdims=True)
        o_ref[:, hs] = _dot(p, v_ref[:, hs]).astype(o_ref.dtype)


def _mem_attn(proj, mem_k, mem_v, q_norm, batch, seq, q_block):
    n_mem, heads, dh = mem_k.shape[1:]
    w = heads * dh
    tt = _tile(seq, 512)
    nt = seq // tt
    return pl.pallas_call(
        functools.partial(_mem_attn_body, heads=heads, dh=dh),
        out_shape=jax.ShapeDtypeStruct((batch * seq, w), BF16),
        grid=(batch, nt),
        in_specs=[
            pl.BlockSpec((tt, w), lambda b, t: (b * nt + t, q_block)),
            pl.BlockSpec((None, n_mem, w), lambda b, t: (b, 0, 0)),
            pl.BlockSpec((None, n_mem, w), lambda b, t: (b, 0, 0)),
            pl.BlockSpec((1, dh), lambda b, t: (0, 0)),
        ],
        out_specs=pl.BlockSpec((tt, w), lambda b, t: (b * nt + t, 0)),
        compiler_params=_params("parallel", "parallel"),
        name="mem_attn",
    )(proj, mem_k.reshape(batch, n_mem, w), mem_v.reshape(batch, n_mem, w), q_norm.reshape(1, dh).astype(F32))


def _merge_body(a0_ref, a1_ref, a2_ref, w0_ref, w1_ref, w2_ref, g0_ref, g1_ref, g2_ref, o_ref):
    acc = _sigmoid(g0_ref[...]) * jnp.dot(a0_ref[...], w0_ref[...], preferred_element_type=F32)
    acc += _sigmoid(g1_ref[...]) * jnp.dot(a1_ref[...], w1_ref[...], preferred_element_type=F32)
    acc += _sigmoid(g2_ref[...]) * jnp.dot(a2_ref[...], w2_ref[...], preferred_element_type=F32)
    o_ref[...] = acc.astype(o_ref.dtype)


def _merge(branches, weights, proj, gate_col, d_model):
    m = proj.shape[0]
    tm = _tile(m, 512)
    tn = _tile(d_model, 512)
    a_specs = [pl.BlockSpec((tm, a.shape[1]), lambda i, j: (i, 0)) for a in branches]
    w_specs = [pl.BlockSpec((w.shape[0], tn), lambda i, j: (0, j)) for w in weights]
    g_specs = [pl.BlockSpec((tm, tn), functools.partial(lambda i, j, off: (i, off + j), off=(gate_col + g * d_model) // tn))
               for g in range(3)]
    return pl.pallas_call(
        _merge_body,
        out_shape=jax.ShapeDtypeStruct((m, d_model), BF16),
        grid=(m // tm, d_model // tn),
        in_specs=a_specs + w_specs + g_specs,
        out_specs=pl.BlockSpec((tm, tn), lambda i, j: (i, j)),
        compiler_params=_params("parallel", "arbitrary"),
        name="merge",
    )(*branches, *weights, proj, proj, proj)


def _ffn_body(h_ref, wg_ref, wu_ref, wo_ref, x_ref, o_ref):
    f = pl.program_id(1)
    h = h_ref[...]
    gate = jnp.dot(h, wg_ref[...], preferred_element_type=F32)
    up = jnp.dot(h, wu_ref[...], preferred_element_type=F32)
    act = (gate * _sigmoid(gate) * up).astype(BF16)
    part = jnp.dot(act, wo_ref[...], preferred_element_type=F32)

    @pl.when(f == 0)
    def _():
        o_ref[...] = x_ref[...] + part

    @pl.when(f > 0)
    def _():
        o_ref[...] += part


def _ffn(h, x, w_in, w_out):
    m, d = x.shape
    dff = w_out.shape[0]
    tm = _tile(m, 512)
    tf = _tile(dff, 512)
    nf = dff // tf
    return pl.pallas_call(
        _ffn_body,
        out_shape=jax.ShapeDtypeStruct((m, d), F32),
        grid=(m // tm, nf),
        in_specs=[
            pl.BlockSpec((tm, d), lambda i, f: (i, 0)),
            pl.BlockSpec((d, tf), lambda i, f: (0, f)),
            pl.BlockSpec((d, tf), lambda i, f: (0, nf + f)),
            pl.BlockSpec((tf, d), lambda i, f: (f, 0)),
            pl.BlockSpec((tm, d), lambda i, f: (i, 0)),
        ],
        out_specs=pl.BlockSpec((tm, d), lambda i, f: (i, 0)),
        compiler_params=_params("parallel", "arbitrary"),
        name="ffn",
    )(h, w_in, w_in, w_out, x)


def _trunk(x, p, sb_attend, conv_prev, s0, mem_k, mem_v, n_valid, precise_gdn):
    batch, seq, d = x.shape
    heads = p["sb_heads"]
    x2 = x.reshape(batch * seq, d)
    h = _rmsnorm(x2, p["ln_mix"])
    proj = _matmul(h, p["w_main"])
    ba = _matmul(h, p["w_ba"])
    o_sb = sb_attend(proj)
    o_gdn, s_new = _gdn(proj, ba, conv_prev, s0, p["w_conv"], p["a_log"], p["dt_bias"], p["gdn_norm"],
                        batch, seq, p["gdn_heads"], qkv_block=1, z_block=p["z_block"],
                        n_valid=n_valid, precise=precise_gdn)
    o_mem = _mem_attn(proj, mem_k, mem_v, p["q_norm_mem"], batch, seq, q_block=p["memq_block"])
    merged = _merge([o_sb, o_gdn, o_mem], [p["w_proj_sb"], p["w_proj_gdn"], p["w_proj_mem"]],
                    proj, p["gate_col"], d)
    x1 = _matmul(merged, p["w_out"], res=x2)
    y = _ffn(_rmsnorm(x1, p["ln_ffn"]), x1, p["w_ffn_in"], p["w_ffn_out"])
    return y.reshape(batch, seq, d), proj, s_new


def kernel(x_prompt, x_sample, mem_prompt, cache_sb_k, cache_sb_v, state_gdn, state_conv, cache_mem_k, cache_mem_v, page_table, ln_mix, w_in, sb_bias, w_conv, a_log, dt_bias, gdn_norm, q_norm_mem, k_norm_mem, ln_mem, w_mem_kv, w_proj_sb, w_proj_gdn, w_proj_mem, w_out, ln_ffn, w_ffn_in, w_ffn_out):
    depth = w_in.shape[0]
    d_model = x_prompt.shape[-1]
    bp, tp = x_prompt.shape[:2]
    bs, ts = x_sample.shape[:2]
    sb_heads = sb_bias.shape[-1]
    gdn_heads = a_log.shape[-1]
    mem_heads, mem_dim = cache_mem_k.shape[-2:]
    n_mem = mem_prompt.shape[1]
    sb_w = sb_heads * SB_DIM
    kw = gdn_heads * GDN_DK
    vw = gdn_heads * GDN_DV
    cw = 2 * kw + vw
    mem_w = mem_heads * mem_dim
    assert sb_w == kw == vw == mem_w and cw == 3 * sb_w and 2 * gdn_heads <= 2 * SUBLANES
    off_ba = 3 * sb_w + cw + vw
    off_memq = off_ba + 2 * gdn_heads
    main_w = off_ba + mem_w + 3 * d_model
    assert w_in.shape[-1] == off_memq + mem_w + 3 * d_model

    yp = x_prompt
    ys = jnp.pad(x_sample, ((0, 0), (0, SAMPLE_PAD - ts), (0, 0)))
    outs = [[] for _ in range(10)]
    for l in range(depth):
        w_main = jnp.concatenate([w_in[l][:, :off_ba], w_in[l][:, off_memq:]], axis=1).astype(BF16)
        w_ba = jnp.pad(w_in[l][:, off_ba:off_memq], ((0, 0), (0, LANES - 2 * gdn_heads))).astype(BF16)
        p = {
            "sb_heads": sb_heads, "gdn_heads": gdn_heads,
            "z_block": (3 * sb_w + cw) // vw, "memq_block": off_ba // mem_w, "gate_col": off_ba + mem_w,
            "ln_mix": ln_mix[l], "w_main": w_main, "w_ba": w_ba, "w_conv": w_conv[l], "a_log": a_log[l],
            "dt_bias": dt_bias[l], "gdn_norm": gdn_norm[l], "q_norm_mem": q_norm_mem[l],
            "w_proj_sb": w_proj_sb[l].astype(BF16), "w_proj_gdn": w_proj_gdn[l].astype(BF16),
            "w_proj_mem": w_proj_mem[l].astype(BF16), "w_out": w_out[l].astype(BF16), "ln_ffn": ln_ffn[l],
            "w_ffn_in": w_ffn_in[l].astype(BF16), "w_ffn_out": w_ffn_out[l].astype(BF16),
        }
        assert main_w == w_main.shape[1]

        kv = _matmul(_rmsnorm(mem_prompt.reshape(bp * n_mem, d_model), ln_mem[l]), w_mem_kv[l].astype(BF16))
        mk = _headnorm(kv, k_norm_mem[l], mem_heads, 0).reshape(bp, n_mem, mem_heads, mem_dim)
        mv = kv[:, mem_w:].reshape(bp, n_mem, mem_heads, mem_dim)
        yp, proj_p, sp = _trunk(
            yp, p, lambda proj: _sb_prompt(proj, sb_bias[l], bp, tp, sb_heads),
            jnp.zeros((bp, SUBLANES, cw), F32), jnp.zeros((bp, gdn_heads, GDN_DK, GDN_DV), F32),
            mk, mv, n_valid=None, precise_gdn=False)

        conv_prev = jnp.pad(state_conv[l], ((0, 0), (SUBLANES - (CONV_W - 1), 0), (0, 0)))
        ys, proj_s, ss = _trunk(
            ys, p, lambda proj: _sb_sample(proj, sb_bias[l], cache_sb_k, cache_sb_v, page_table, l, sb_heads),
            conv_prev, state_gdn[l], cache_mem_k[l], cache_mem_v[l], n_valid=ts, precise_gdn=True)

        pp = proj_p.reshape(bp, tp, main_w)
        ps = proj_s.reshape(bs, SAMPLE_PAD, main_w)
        outs[0].append(pp[:, :, sb_w:2 * sb_w].reshape(bp, tp, sb_heads, SB_DIM))
        outs[1].append(pp[:, :, 2 * sb_w:3 * sb_w].reshape(bp, tp, sb_heads, SB_DIM))
        outs[2].append(ps[:, :ts, sb_w:2 * sb_w].reshape(bs, ts, sb_heads, SB_DIM))
        outs[3].append(ps[:, :ts, 2 * sb_w:3 * sb_w].reshape(bs, ts, sb_heads, SB_DIM))
        outs[4].append(sp)
        outs[5].append(ss)
        outs[6].append(pp[:, tp - (CONV_W - 1):, 3 * sb_w:3 * sb_w + cw])
        outs[7].append(ps[:, ts - (CONV_W - 1):ts, 3 * sb_w:3 * sb_w + cw])
        outs[8].append(mk)
        outs[9].append(mv)
    return (yp, ys[:, :ts]) + tuple(jnp.stack(o) for o in outs)
```

```python
import functools

import jax
import jax.numpy as jnp
from jax import lax
from jax.experimental import pallas as pl
from jax.experimental.pallas import tpu as pltpu

F32 = jnp.float32
BF16 = jnp.bfloat16

EPS = 1e-6
LOG2E = 1.4426950408889634
SB_DIM = 128
SB_RUN = 256
GDN_DK = 128
GDN_DV = 128
GDN_CHUNK = 64
CONV_W = 4
SUBLANES = 8
LANES = 128
SAMPLE_PAD = 2 * SUBLANES
VMEM_LIMIT = 56 * 1024 * 1024


def _params(*sem):
    return pltpu.CompilerParams(dimension_semantics=sem, vmem_limit_bytes=VMEM_LIMIT)


def _tile(n, pref):
    if n <= pref:
        return n
    t = pref
    while n % t:
        t //= 2
    return t


def _dot(a, b):
    return jnp.dot(a.astype(BF16), b.astype(BF16), preferred_element_type=F32)


def _dot_nt(a, b):
    return lax.dot_general(a.astype(BF16), b.astype(BF16), (((1,), (1,)), ((), ())), preferred_element_type=F32)


def _sigmoid(x):
    return 1.0 / (1.0 + jnp.exp(-x))


def _softplus(x):
    return jnp.maximum(x, 0.0) + jnp.log(1.0 + jnp.exp(-jnp.abs(x)))


def _rmsnorm_body(x_ref, g_ref, o_ref):
    x = x_ref[...].astype(F32)
    ms = jnp.mean(x * x, axis=-1, keepdims=True)
    o_ref[...] = (x * lax.rsqrt(ms + EPS) * g_ref[...]).astype(o_ref.dtype)


def _rmsnorm(x, g, out_dtype=BF16):
    m, d = x.shape
    tm = _tile(m, 512)
    return pl.pallas_call(
        _rmsnorm_body,
        out_shape=jax.ShapeDtypeStruct((m, d), out_dtype),
        grid=(m // tm,),
        in_specs=[pl.BlockSpec((tm, d), lambda i: (i, 0)), pl.BlockSpec((1, d), lambda i: (0, 0))],
        out_specs=pl.BlockSpec((tm, d), lambda i: (i, 0)),
        compiler_params=_params("parallel"),
        name="rmsnorm",
    )(x, g.reshape(1, d).astype(F32))


def _headnorm_body(x_ref, g_ref, o_ref, *, heads, dh):
    for h in range(heads):
        x = x_ref[:, h * dh:(h + 1) * dh].astype(F32)
        ms = jnp.mean(x * x, axis=-1, keepdims=True)
        o_ref[:, h * dh:(h + 1) * dh] = (x * lax.rsqrt(ms + EPS) * g_ref[...]).astype(o_ref.dtype)


def _headnorm(x, g, heads, col_block):
    m = x.shape[0]
    dh = g.shape[-1]
    w = heads * dh
    tm = _tile(m, 512)
    return pl.pallas_call(
        functools.partial(_headnorm_body, heads=heads, dh=dh),
        out_shape=jax.ShapeDtypeStruct((m, w), F32),
        grid=(m // tm,),
        in_specs=[pl.BlockSpec((tm, w), lambda i: (i, col_block)), pl.BlockSpec((1, dh), lambda i: (0, 0))],
        out_specs=pl.BlockSpec((tm, w), lambda i: (i, 0)),
        compiler_params=_params("parallel"),
        name="headnorm",
    )(x, g.reshape(1, dh).astype(F32))


def _mm_body(a_ref, b_ref, o_ref):
    o_ref[...] = jnp.dot(a_ref[...], b_ref[...], preferred_element_type=F32).astype(o_ref.dtype)


def _mm_res_body(a_ref, b_ref, r_ref, o_ref):
    o_ref[...] = (r_ref[...] + jnp.dot(a_ref[...], b_ref[...], preferred_element_type=F32)).astype(o_ref.dtype)


def _matmul(a, b, res=None, out_dtype=F32):
    m, k = a.shape
    n = b.shape[1]
    tm = _tile(m, 1024)
    tn = _tile(n, 512)
    in_specs = [pl.BlockSpec((tm, k), lambda i, j: (i, 0)), pl.BlockSpec((k, tn), lambda i, j: (0, j))]
    args = [a, b]
    body = _mm_body
    if res is not None:
        in_specs.append(pl.BlockSpec((tm, tn), lambda i, j: (i, j)))
        args.append(res)
        body = _mm_res_body
    return pl.pallas_call(
        body,
        out_shape=jax.ShapeDtypeStruct((m, n), out_dtype),
        grid=(m // tm, n // tn),
        in_specs=in_specs,
        out_specs=pl.BlockSpec((tm, tn), lambda i, j: (i, j)),
        compiler_params=_params("parallel", "arbitrary"),
        name="matmul",
    )(*args)


def _sb_runs(items, carries, suffix):
    n = len(items)
    scores, sums, out = {}, {}, {}

    def stack(parts):
        return parts[0] if len(parts) == 1 else jnp.concatenate(parts, axis=0)

    def emit_scores(i):
        _, qs, ks, _, biases, _ = items[i]
        scores[i] = stack([_dot_nt(q, k) + bias for q, k, bias in zip(qs, ks, biases)])

    def emit_sums(i):
        mask = items[i][5]
        z = scores.pop(i)
        neg_abs = lax.bitcast_convert_type(lax.bitcast_convert_type(z, jnp.uint32) | jnp.uint32(0x80000000), F32)
        sp = jnp.maximum(z, 0.0) + jnp.log2(1.0 + jnp.exp2(neg_abs))
        if mask is not None:
            sp = jnp.where(mask, sp, 0.0)
        hi = sp.astype(BF16)
        lo = (sp - hi.astype(F32)).astype(BF16)
        incl = jnp.dot(jnp.concatenate([hi, lo], axis=1), suffix, preferred_element_type=F32)
        sums[i] = (z, incl)

    def emit_values(i):
        group, qs, _, vs, _, mask = items[i]
        z, incl = sums.pop(i)
        carry = carries[group]
        a = jnp.exp2(z - incl - jnp.concatenate([carry] * (z.shape[1] // LANES), axis=1))
        if mask is not None:
            a = jnp.where(mask, a, 0.0)
        carries[group] = carry + jnp.broadcast_to(incl[:, 0:1], carry.shape)
        a = a.astype(BF16)
        r = qs[0].shape[0]
        contrib = stack([jnp.dot(a[j * r:(j + 1) * r], v, preferred_element_type=F32) for j, v in enumerate(vs)])
        out[group] = contrib if group not in out else out[group] + contrib

    for step in range(n + 2):
        if step < n:
            emit_scores(step)
        if 0 <= step - 1 < n:
            emit_sums(step - 1)
        if 0 <= step - 2 < n:
            emit_values(step - 2)
    return out


def _suffix_sum_operand(run):
    j = lax.broadcasted_iota(jnp.int32, (2 * run, run), 0) % run
    s = lax.broadcasted_iota(jnp.int32, (2 * run, run), 1)
    return jnp.where(j >= s, 1.0, 0.0).astype(BF16)


def _sb_prompt_body(bias_ref, q_ref, k_ref, v_ref, suffix_ref, o_ref, kb_ref, vb_ref, qs_ref, acc_ref, carry_ref,
                    *, tq, rb, kb, scale):
    h = pl.program_id(1)
    qi = pl.program_id(2)
    run = suffix_ref.shape[1]

    @pl.when(qi == 0)
    def _():
        kb_ref[...] = k_ref[...].astype(BF16)
        vb_ref[...] = v_ref[...].astype(BF16)

    qs_ref[...] = (q_ref[...] * (scale * LOG2E)).astype(BF16)
    acc_ref[...] = jnp.zeros_like(acc_ref)
    carry_ref[...] = jnp.zeros_like(carry_ref)
    bias = bias_ref[h] * LOG2E
    suffix = suffix_ref[...]
    groups = list(range(tq // rb))

    def update(first_run, n, diagonal):
        items = []
        for r in reversed(range(n)):
            start = pl.multiple_of((first_run + r) * run, run)
            k = kb_ref[pl.ds(start, run), :]
            v = vb_ref[pl.ds(start, run), :]
            for g in groups:
                if diagonal and r > g:
                    continue
                items.append((g, [qs_ref[g * rb:(g + 1) * rb]], [k], [v], [bias],
                              causal if diagonal and r == g else None))
        carries = {g: carry_ref[g * rb:(g + 1) * rb] for g in groups}
        for g, contrib in _sb_runs(items, carries, suffix).items():
            acc_ref[g * rb:(g + 1) * rb] += contrib
            carry_ref[g * rb:(g + 1) * rb] = carries[g]

    causal = lax.broadcasted_iota(jnp.int32, (rb, run), 1) < lax.broadcasted_iota(jnp.int32, (rb, run), 0)
    update(qi * (tq // run), tq // run, True)

    n_trips = (qi * tq) // (kb * run)

    def body(jj, c):
        update((n_trips - 1 - jj) * kb, kb, False)
        return c

    lax.fori_loop(0, n_trips, body, 0)
    o_ref[...] = acc_ref[...].astype(o_ref.dtype)


def _sb_prompt(proj, bias, batch, seq, heads, tq=1024, kb=2):
    d = SB_DIM
    run = SB_RUN
    rb = run
    tq = _tile(seq, tq)
    kb = min(kb, tq // run)
    assert tq % (kb * run) == 0 and tq % rb == 0
    nq = seq // tq
    return pl.pallas_call(
        functools.partial(_sb_prompt_body, tq=tq, rb=rb, kb=kb, scale=d ** -0.5),
        out_shape=jax.ShapeDtypeStruct((batch * seq, heads * d), BF16),
        grid=(batch, heads, nq),
        in_specs=[
            pl.BlockSpec(memory_space=pltpu.SMEM),
            pl.BlockSpec((tq, d), lambda b, h, i: (b * nq + i, h)),
            pl.BlockSpec((seq, d), lambda b, h, i: (b, heads + h)),
            pl.BlockSpec((seq, d), lambda b, h, i: (b, 2 * heads + h)),
            pl.BlockSpec((2 * run, run), lambda b, h, i: (0, 0)),
        ],
        out_specs=pl.BlockSpec((tq, d), lambda b, h, i: (b * nq + i, h)),
        scratch_shapes=[pltpu.VMEM((seq, d), BF16), pltpu.VMEM((seq, d), BF16), pltpu.VMEM((tq, d), BF16),
                        pltpu.VMEM((tq, d), F32), pltpu.VMEM((tq, LANES), F32)],
        compiler_params=_params("parallel", "parallel", "arbitrary"),
        name="sb_prompt",
    )(bias.astype(F32), proj, proj, proj, _suffix_sum_operand(run))


def _sb_sample_body(pt_ref, bias_ref, q_ref, kn_ref, vn_ref, suffix_ref, *refs, heads, pages_per_step, page, scale):
    del pt_ref
    page_refs = refs[:2 * pages_per_step]
    o_ref, acc_ref, carry_ref = refs[2 * pages_per_step:]
    s = pl.program_id(1)
    d = SB_DIM
    run = suffix_ref.shape[1]
    qp = q_ref.shape[0]
    q = (q_ref[...] * (scale * LOG2E)).astype(BF16)
    suffix = suffix_ref[...]
    qs = [q[:, h * d:(h + 1) * d] for h in range(heads)]
    biases = [bias_ref[h] * LOG2E for h in range(heads)]

    @pl.when(s == 0)
    def _():
        pad = jnp.zeros((run - qp, heads * d), F32)
        k_new = jnp.concatenate([kn_ref[...], pad], axis=0).astype(BF16)
        v_new = jnp.concatenate([vn_ref[...], pad], axis=0).astype(BF16)
        row = lax.broadcasted_iota(jnp.int32, (heads * qp, run), 0) % qp
        mask = lax.broadcasted_iota(jnp.int32, (heads * qp, run), 1) < row
        item = (0, qs, [k_new[:, h * d:(h + 1) * d] for h in range(heads)],
                [v_new[:, h * d:(h + 1) * d] for h in range(heads)], biases, mask)
        carries = {0: jnp.zeros((heads * qp, LANES), F32)}
        acc_ref[...] = _sb_runs([item], carries, suffix)[0]
        carry_ref[...] = carries[0]

    def head_rows(ref, h):
        return ref[pl.ds(h, page, stride=heads), :].astype(BF16)

    ppr = run // page
    items = []
    for first in range(0, pages_per_step, ppr):
        order = list(reversed(range(first, first + ppr)))
        ks = [jnp.concatenate([head_rows(page_refs[2 * i], h) for i in order], axis=0) for h in range(heads)]
        vs = [jnp.concatenate([head_rows(page_refs[2 * i + 1], h) for i in order], axis=0) for h in range(heads)]
        items.append((0, qs, ks, vs, biases, None))
    carries = {0: carry_ref[...]}
    acc_ref[...] += _sb_runs(items, carries, suffix)[0]
    carry_ref[...] = carries[0]

    @pl.when(s == pl.num_programs(1) - 1)
    def _():
        for h in range(heads):
            o_ref[:, h * d:(h + 1) * d] = acc_ref[h * qp:(h + 1) * qp, :].astype(o_ref.dtype)


def _sb_sample(proj, bias, cache_k, cache_v, page_table, layer, heads):
    d = SB_DIM
    run = SB_RUN
    batch, n_pages = page_table.shape
    depth, n_phys, page = cache_k.shape[:3]
    w = heads * d
    qp = SAMPLE_PAD
    pps = _tile(n_pages, 8)
    assert (pps * page) % run == 0
    steps = n_pages // pps
    ck = cache_k.reshape(depth * n_phys, page * heads, d)
    cv = cache_v.reshape(depth * n_phys, page * heads, d)

    def page_spec(i):
        def index(b, s, pt):
            return (layer * n_phys + pt[b * n_pages + (n_pages - 1 - (s * pps + i))], 0, 0)
        return pl.BlockSpec((None, page * heads, d), index)

    page_specs = []
    page_args = []
    for i in range(pps):
        page_specs += [page_spec(i), page_spec(i)]
        page_args += [ck, cv]
    grid_spec = pltpu.PrefetchScalarGridSpec(
        num_scalar_prefetch=1,
        grid=(batch, steps),
        in_specs=[
            pl.BlockSpec(memory_space=pltpu.SMEM),
            pl.BlockSpec((qp, w), lambda b, s, pt: (b, 0)),
            pl.BlockSpec((qp, w), lambda b, s, pt: (b, 1)),
            pl.BlockSpec((qp, w), lambda b, s, pt: (b, 2)),
            pl.BlockSpec((2 * run, run), lambda b, s, pt: (0, 0)),
        ] + page_specs,
        out_specs=pl.BlockSpec((qp, w), lambda b, s, pt: (b, 0)),
        scratch_shapes=[pltpu.VMEM((heads * qp, d), F32), pltpu.VMEM((heads * qp, LANES), F32)],
    )
    return pl.pallas_call(
        functools.partial(_sb_sample_body, heads=heads, pages_per_step=pps, page=page, scale=d ** -0.5),
        out_shape=jax.ShapeDtypeStruct((batch * qp, w), BF16),
        grid_spec=grid_spec,
        compiler_params=_params("parallel", "arbitrary"),
        name="sb_sample",
    )(page_table.reshape(-1).astype(jnp.int32), bias.astype(F32), proj, proj, proj,
      _suffix_sum_operand(run), *page_args)


_NN = (((1,), (0,)), ((), ()))
_NT = (((1,), (1,)), ((), ()))
_TN = (((0,), (0,)), ((), ()))


def _split_bf16(x):
    hi = x.astype(BF16)
    return hi, (x - hi.astype(F32)).astype(BF16)


def _mm(a, b, dims, precise):
    dot = lambda x, y: lax.dot_general(x, y, dims, preferred_element_type=F32)
    if not precise:
        return dot(a.astype(BF16), b.astype(BF16))
    ah, al = _split_bf16(a)
    bh, bl = _split_bf16(b)
    return dot(ah, bh) + dot(ah, bl) + dot(al, bh)


def _split3_bf16(x):
    hi = x.astype(BF16)
    rest = x - hi.astype(F32)
    mid = rest.astype(BF16)
    return hi, mid, (rest - mid.astype(F32)).astype(BF16)


def _mm_exact_lhs(a, b):
    a = a.astype(BF16)
    return sum(jnp.dot(a, piece, preferred_element_type=F32) for piece in _split3_bf16(b))


def _mm_exact_rhs(a, b):
    b = b.astype(BF16)
    return sum(jnp.dot(piece, b, preferred_element_type=F32) for piece in _split3_bf16(a))


def _unit_lower_inverses(ms):
    c = ms[0].shape[0]
    eye = (lax.broadcasted_iota(jnp.int32, (c, c), 0) == lax.broadcasted_iota(jnp.int32, (c, c), 1)).astype(F32)
    ps = [-m for m in ms]
    invs = [eye + p for p in ps]
    span = 2
    while span < c:
        ps = [_mm(p, p, _NN, True) for p in ps]
        invs = [inv + _mm(inv, p, _NN, True) for inv, p in zip(invs, ps)]
        span *= 2
    return invs


def _gdn_body(qkv_ref, z_ref, ba_ref, bat_ref, prev_ref, s0_ref, wconv_ref, arow_ref, dtrow_ref,
              acol_ref, dtcol_ref, gnorm_ref, o_ref, s_ref, tail_ref, *, heads, chunk, n_valid, precise):
    t = pl.program_id(1)
    tt = qkv_ref.shape[0]
    kw = heads * GDN_DK
    sub = SUBLANES

    @pl.when(t == 0)
    def _():
        tail_ref[...] = prev_ref[...]
        s_ref[...] = s0_ref[...]

    x = qkv_ref[...]
    head_rows = jnp.concatenate([tail_ref[...], x[:sub]], axis=0)
    y = x * wconv_ref[CONV_W - 1:CONV_W, :]
    for shift in range(1, CONV_W):
        first = pltpu.roll(head_rows, shift, 0)[sub:]
        if tt > sub:
            shifted = jnp.concatenate([first, pltpu.roll(x, shift, 0)[sub:]], axis=0)
        else:
            shifted = first
        y = y + shifted * wconv_ref[CONV_W - 1 - shift:CONV_W - shift, :]
    tail_ref[...] = x[tt - sub:]
    y = y * _sigmoid(y)

    n_chunks = max(tt, chunk) // chunk

    def rows(a, c):
        if tt < chunk:
            return jnp.concatenate([a, jnp.zeros((chunk - tt, a.shape[1]), a.dtype)], axis=0)
        return a[c * chunk:(c + 1) * chunk]

    ri = lax.broadcasted_iota(jnp.int32, (chunk, chunk), 0)
    ci = lax.broadcasted_iota(jnp.int32, (chunk, chunk), 1)
    incl = ri >= ci
    strict = ri > ci
    tri = incl.astype(F32)
    tri_t = (ri <= ci).astype(F32)

    items = []
    for c in range(n_chunks):
        ba = rows(ba_ref[...], c)
        g_tok = -jnp.exp(arow_ref[...]) * _softplus(ba + dtrow_ref[...])
        beta_tok = _sigmoid(ba)
        bat = bat_ref[:, c * chunk:(c + 1) * chunk]
        g_row = -jnp.exp(acol_ref[...]) * _softplus(bat + dtcol_ref[...])
        if n_valid is not None:
            valid_r = lax.broadcasted_iota(jnp.int32, g_tok.shape, 0) < n_valid
            g_tok = jnp.where(valid_r, g_tok, 0.0)
            beta_tok = jnp.where(valid_r, beta_tok, 0.0)
            g_row = jnp.where(lax.broadcasted_iota(jnp.int32, g_row.shape, 1) < n_valid, g_row, 0.0)
        gc_tok = _mm_exact_lhs(tri, g_tok)
        gc_row = _mm_exact_rhs(g_row, tri_t)
        eg_tok = jnp.exp(gc_tok)
        yc = rows(y, c)
        for h in range(heads):
            gcol = gc_tok[:, heads + h:heads + h + 1]
            grow = gc_row[heads + h:heads + h + 1, :]
            beta = beta_tok[:, h:h + 1]
            eg = eg_tok[:, heads + h:heads + h + 1]
            q = yc[:, h * GDN_DK:(h + 1) * GDN_DK]
            k = yc[:, kw + h * GDN_DK:kw + (h + 1) * GDN_DK]
            v = yc[:, 2 * kw + h * GDN_DV:2 * kw + (h + 1) * GDN_DV]
            q = q * lax.rsqrt(jnp.sum(q * q, axis=-1, keepdims=True) + EPS) * (GDN_DK ** -0.5)
            k = k * lax.rsqrt(jnp.sum(k * k, axis=-1, keepdims=True) + EPS)
            g_last = gcol[chunk - 1:chunk, :]
            items.append(dict(
                c=c, h=h, q=q, k=k, kb=k * beta, g_last=g_last,
                decay=jnp.where(incl, jnp.exp(jnp.where(incl, gcol - grow, 0.0)), 0.0),
                rhs=jnp.concatenate([v * beta, (k * beta) * eg], axis=1),
                q_dec=q * eg, k_dec=k * jnp.exp(g_last - gcol)))
    kks = [_mm(it["kb"], it["k"], _NT, precise) for it in items]
    qks = [_mm(it["q"], it["k"], _NT, precise) for it in items]
    invs = _unit_lower_inverses([jnp.where(strict, kk * it["decay"], 0.0) for kk, it in zip(kks, items)])
    uws = [_mm(inv, it["rhs"], _NN, True) for inv, it in zip(invs, items)]
    qks = [jnp.where(incl, qk * it["decay"], 0.0) for qk, it in zip(qks, items)]

    for c in range(n_chunks):
        sel = [i for i, it in enumerate(items) if it["c"] == c]
        states = [s_ref[items[i]["h"]] for i in sel]
        ws = [_mm(jnp.concatenate([uws[i][:, GDN_DV:], items[i]["q_dec"]], axis=0), s, _NN, precise)
              for i, s in zip(sel, states)]
        v_news = [uws[i][:, :GDN_DV] - w[:chunk] for i, w in zip(sel, ws)]
        outs = [w[chunk:] + _mm(qks[i], vn, _NN, precise) for i, w, vn in zip(sel, ws, v_news)]
        deltas = [_mm(items[i]["k_dec"], vn, _TN, precise) for i, vn in zip(sel, v_news)]
        zc = rows(z_ref[...], c)
        for i, s, o, delta in zip(sel, states, outs, deltas):
            h = items[i]["h"]
            s_ref[h] = s * jnp.exp(items[i]["g_last"]) + delta
            o = o * lax.rsqrt(jnp.mean(o * o, axis=-1, keepdims=True) + EPS) * gnorm_ref[...]
            zz = zc[:, h * GDN_DV:(h + 1) * GDN_DV]
            out = (o * (zz * _sigmoid(zz))).astype(o_ref.dtype)
            if tt < chunk:
                o_ref[:, h * GDN_DV:(h + 1) * GDN_DV] = out[:tt]
            else:
                o_ref[c * chunk:(c + 1) * chunk, h * GDN_DV:(h + 1) * GDN_DV] = out


def _gdn(proj, ba, prev, s0, w_conv, a_log, dt_bias, gdn_norm, batch, seq, heads, qkv_block, z_block,
         n_valid, precise):
    kw = heads * GDN_DK
    vw = heads * GDN_DV
    cw = 2 * kw + vw
    chunk = GDN_CHUNK
    tt = _tile(seq, 128)
    assert tt % chunk == 0 or tt < chunk
    nt = seq // tt
    bat = jnp.swapaxes(ba[:, :2 * SUBLANES].reshape(batch, seq, 2 * SUBLANES), 1, 2)
    if seq < chunk:
        bat = jnp.pad(bat, ((0, 0), (0, 0), (0, chunk - seq)))
    arow = jnp.zeros((1, LANES), F32).at[0, heads:2 * heads].set(a_log.astype(F32))
    dtrow = jnp.zeros((1, LANES), F32).at[0, heads:2 * heads].set(dt_bias.astype(F32))
    acol = arow[0, :2 * SUBLANES].reshape(2 * SUBLANES, 1)
    dtcol = dtrow[0, :2 * SUBLANES].reshape(2 * SUBLANES, 1)
    const = lambda b, t: (0, 0)
    o, s_new = pl.pallas_call(
        functools.partial(_gdn_body, heads=heads, chunk=chunk, n_valid=n_valid, precise=precise),
        out_shape=(jax.ShapeDtypeStruct((batch * seq, vw), BF16),
                   jax.ShapeDtypeStruct((batch, heads, GDN_DK, GDN_DV), F32)),
        grid=(batch, nt),
        in_specs=[
            pl.BlockSpec((tt, cw), lambda b, t: (b * nt + t, qkv_block)),
            pl.BlockSpec((tt, vw), lambda b, t: (b * nt + t, z_block)),
            pl.BlockSpec((tt, LANES), lambda b, t: (b * nt + t, 0)),
            pl.BlockSpec((None, 2 * SUBLANES, max(tt, chunk)), lambda b, t: (b, 0, t)),
            pl.BlockSpec((None, SUBLANES, cw), lambda b, t: (b, 0, 0)),
            pl.BlockSpec((None, heads, GDN_DK, GDN_DV), lambda b, t: (b, 0, 0, 0)),
            pl.BlockSpec((CONV_W, cw), const),
            pl.BlockSpec((1, LANES), const),
            pl.BlockSpec((1, LANES), const),
            pl.BlockSpec((2 * SUBLANES, 1), const),
            pl.BlockSpec((2 * SUBLANES, 1), const),
            pl.BlockSpec((1, GDN_DV), const),
        ],
        out_specs=(pl.BlockSpec((tt, vw), lambda b, t: (b * nt + t, 0)),
                   pl.BlockSpec((None, heads, GDN_DK, GDN_DV), lambda b, t: (b, 0, 0, 0))),
        scratch_shapes=[pltpu.VMEM((SUBLANES, cw), F32)],
        compiler_params=_params("parallel", "arbitrary"),
        name="gdn",
    )(proj, proj, ba, bat, prev, s0, w_conv.astype(F32), arow, dtrow, acol, dtcol,
      gdn_norm.reshape(1, GDN_DV).astype(F32))
    return o, s_new


def _mem_attn_body(q_ref, k_ref, v_ref, g_ref, o_ref, *, heads, dh):
    scale = dh ** -0.5
    for h in range(heads):
        hs = slice(h * dh, (h + 1) * dh)
        q = q_ref[:, hs]
        q = q * lax.rsqrt(jnp.mean(q * q, axis=-1, keepdims=True) + EPS) * g_ref[...]
        s = _dot_nt(q, k_ref[:, hs]) * scale
        e = jnp.exp(s - jnp.max(s, axis=-1, keepdims=True))
        p = e / jnp.sum(e, axis=-1, keepdims=True)
        o_ref[:, hs] = _dot(p, v_ref[:, hs]).astype(o_ref.dtype)


def _mem_attn(proj, mem_k, mem_v, q_norm, batch, seq, q_block):
    n_mem, heads, dh = mem_k.shape[1:]
    w = heads * dh
    tt = _tile(seq, 512)
    nt = seq // tt
    return pl.pallas_call(
        functools.partial(_mem_attn_body, heads=heads, dh=dh),
        out_shape=jax.ShapeDtypeStruct((batch * seq, w), BF16),
        grid=(batch, nt),
        in_specs=[
            pl.BlockSpec((tt, w), lambda b, t: (b * nt + t, q_block)),
            pl.BlockSpec((None, n_mem, w), lambda b, t: (b, 0, 0)),
            pl.BlockSpec((None, n_mem, w), lambda b, t: (b, 0, 0)),
            pl.BlockSpec((1, dh), lambda b, t: (0, 0)),
        ],
        out_specs=pl.BlockSpec((tt, w), lambda b, t: (b * nt + t, 0)),
        compiler_params=_params("parallel", "parallel"),
        name="mem_attn",
    )(proj, mem_k.reshape(batch, n_mem, w), mem_v.reshape(batch, n_mem, w), q_norm.reshape(1, dh).astype(F32))


def _merge_body(a0_ref, a1_ref, a2_ref, w0_ref, w1_ref, w2_ref, g0_ref, g1_ref, g2_ref, o_ref):
    acc = _sigmoid(g0_ref[...]) * jnp.dot(a0_ref[...], w0_ref[...], preferred_element_type=F32)
    acc += _sigmoid(g1_ref[...]) * jnp.dot(a1_ref[...], w1_ref[...], preferred_element_type=F32)
    acc += _sigmoid(g2_ref[...]) * jnp.dot(a2_ref[...], w2_ref[...], preferred_element_type=F32)
    o_ref[...] = acc.astype(o_ref.dtype)


def _merge(branches, weights, proj, gate_col, d_model):
    m = proj.shape[0]
    tm = _tile(m, 512)
    tn = _tile(d_model, 512)
    a_specs = [pl.BlockSpec((tm, a.shape[1]), lambda i, j: (i, 0)) for a in branches]
    w_specs = [pl.BlockSpec((w.shape[0], tn), lambda i, j: (0, j)) for w in weights]
    g_specs = [pl.BlockSpec((tm, tn), functools.partial(lambda i, j, off: (i, off + j), off=(gate_col + g * d_model) // tn))
               for g in range(3)]
    return pl.pallas_call(
        _merge_body,
        out_shape=jax.ShapeDtypeStruct((m, d_model), BF16),
        grid=(m // tm, d_model // tn),
        in_specs=a_specs + w_specs + g_specs,
        out_specs=pl.BlockSpec((tm, tn), lambda i, j: (i, j)),
        compiler_params=_params("parallel", "arbitrary"),
        name="merge",
    )(*branches, *weights, proj, proj, proj)


def _ffn_body(h_ref, wg_ref, wu_ref, wo_ref, x_ref, o_ref):
    f = pl.program_id(1)
    h = h_ref[...]
    gate = jnp.dot(h, wg_ref[...], preferred_element_type=F32)
    up = jnp.dot(h, wu_ref[...], preferred_element_type=F32)
    act = (gate * _sigmoid(gate) * up).astype(BF16)
    part = jnp.dot(act, wo_ref[...], preferred_element_type=F32)

    @pl.when(f == 0)
    def _():
        o_ref[...] = x_ref[...] + part

    @pl.when(f > 0)
    def _():
        o_ref[...] += part


def _ffn(h, x, w_in, w_out):
    m, d = x.shape
    dff = w_out.shape[0]
    tm = _tile(m, 512)
    tf = _tile(dff, 512)
    nf = dff // tf
    return pl.pallas_call(
        _ffn_body,
        out_shape=jax.ShapeDtypeStruct((m, d), F32),
        grid=(m // tm, nf),
        in_specs=[
            pl.BlockSpec((tm, d), lambda i, f: (i, 0)),
            pl.BlockSpec((d, tf), lambda i, f: (0, f)),
            pl.BlockSpec((d, tf), lambda i, f: (0, nf + f)),
            pl.BlockSpec((tf, d), lambda i, f: (f, 0)),
            pl.BlockSpec((tm, d), lambda i, f: (i, 0)),
        ],
        out_specs=pl.BlockSpec((tm, d), lambda i, f: (i, 0)),
        compiler_params=_params("parallel", "arbitrary"),
        name="ffn",
    )(h, w_in, w_in, w_out, x)


def _trunk(x, p, sb_attend, conv_prev, s0, mem_k, mem_v, n_valid, precise_gdn):
    batch, seq, d = x.shape
    heads = p["sb_heads"]
    x2 = x.reshape(batch * seq, d)
    h = _rmsnorm(x2, p["ln_mix"])
    proj = _matmul(h, p["w_main"])
    ba = _matmul(h, p["w_ba"])
    o_sb = sb_attend(proj)
    o_gdn, s_new = _gdn(proj, ba, conv_prev, s0, p["w_conv"], p["a_log"], p["dt_bias"], p["gdn_norm"],
                        batch, seq, p["gdn_heads"], qkv_block=1, z_block=p["z_block"],
                        n_valid=n_valid, precise=precise_gdn)
    o_mem = _mem_attn(proj, mem_k, mem_v, p["q_norm_mem"], batch, seq, q_block=p["memq_block"])
    merged = _merge([o_sb, o_gdn, o_mem], [p["w_proj_sb"], p["w_proj_gdn"], p["w_proj_mem"]],
                    proj, p["gate_col"], d)
    x1 = _matmul(merged, p["w_out"], res=x2)
    y = _ffn(_rmsnorm(x1, p["ln_ffn"]), x1, p["w_ffn_in"], p["w_ffn_out"])
    return y.reshape(batch, seq, d), proj, s_new


def kernel(x_prompt, x_sample, mem_prompt, cache_sb_k, cache_sb_v, state_gdn, state_conv, cache_mem_k, cache_mem_v, page_table, ln_mix, w_in, sb_bias, w_conv, a_log, dt_bias, gdn_norm, q_norm_mem, k_norm_mem, ln_mem, w_mem_kv, w_proj_sb, w_proj_gdn, w_proj_mem, w_out, ln_ffn, w_ffn_in, w_ffn_out):
    depth = w_in.shape[0]
    d_model = x_prompt.shape[-1]
    bp, tp = x_prompt.shape[:2]
    bs, ts = x_sample.shape[:2]
    sb_heads = sb_bias.shape[-1]
    gdn_heads = a_log.shape[-1]
    mem_heads, mem_dim = cache_mem_k.shape[-2:]
    n_mem = mem_prompt.shape[1]
    sb_w = sb_heads * SB_DIM
    kw = gdn_heads * GDN_DK
    vw = gdn_heads * GDN_DV
    cw = 2 * kw + vw
    mem_w = mem_heads * mem_dim
    assert sb_w == kw == vw == mem_w and cw == 3 * sb_w and 2 * gdn_heads <= 2 * SUBLANES
    off_ba = 3 * sb_w + cw + vw
    off_memq = off_ba + 2 * gdn_heads
    main_w = off_ba + mem_w + 3 * d_model
    assert w_in.shape[-1] == off_memq + mem_w + 3 * d_model

    yp = x_prompt
    ys = jnp.pad(x_sample, ((0, 0), (0, SAMPLE_PAD - ts), (0, 0)))
    outs = [[] for _ in range(10)]
    for l in range(depth):
        w_main = jnp.concatenate([w_in[l][:, :off_ba], w_in[l][:, off_memq:]], axis=1).astype(BF16)
        w_ba = jnp.pad(w_in[l][:, off_ba:off_memq], ((0, 0), (0, LANES - 2 * gdn_heads))).astype(BF16)
        p = {
            "sb_heads": sb_heads, "gdn_heads": gdn_heads,
            "z_block": (3 * sb_w + cw) // vw, "memq_block": off_ba // mem_w, "gate_col": off_ba + mem_w,
            "ln_mix": ln_mix[l], "w_main": w_main, "w_ba": w_ba, "w_conv": w_conv[l], "a_log": a_log[l],
            "dt_bias": dt_bias[l], "gdn_norm": gdn_norm[l], "q_norm_mem": q_norm_mem[l],
            "w_proj_sb": w_proj_sb[l].astype(BF16), "w_proj_gdn": w_proj_gdn[l].astype(BF16),
            "w_proj_mem": w_proj_mem[l].astype(BF16), "w_out": w_out[l].astype(BF16), "ln_ffn": ln_ffn[l],
            "w_ffn_in": w_ffn_in[l].astype(BF16), "w_ffn_out": w_ffn_out[l].astype(BF16),
        }
        assert main_w == w_main.shape[1]

        kv = _matmul(_rmsnorm(mem_prompt.reshape(bp * n_mem, d_model), ln_mem[l]), w_mem_kv[l].astype(BF16))
        mk = _headnorm(kv, k_norm_mem[l], mem_heads, 0).reshape(bp, n_mem, mem_heads, mem_dim)
        mv = kv[:, mem_w:].reshape(bp, n_mem, mem_heads, mem_dim)
        yp, proj_p, sp = _trunk(
            yp, p, lambda proj: _sb_prompt(proj, sb_bias[l], bp, tp, sb_heads),
            jnp.zeros((bp, SUBLANES, cw), F32), jnp.zeros((bp, gdn_heads, GDN_DK, GDN_DV), F32),
            mk, mv, n_valid=None, precise_gdn=False)

        conv_prev = jnp.pad(state_conv[l], ((0, 0), (SUBLANES - (CONV_W - 1), 0), (0, 0)))
        ys, proj_s, ss = _trunk(
            ys, p, lambda proj: _sb_sample(proj, sb_bias[l], cache_sb_k, cache_sb_v, page_table, l, sb_heads),
            conv_prev, state_gdn[l], cache_mem_k[l], cache_mem_v[l], n_valid=ts, precise_gdn=True)

        pp = proj_p.reshape(bp, tp, main_w)
        ps = proj_s.reshape(bs, SAMPLE_PAD, main_w)
        outs[0].append(pp[:, :, sb_w:2 * sb_w].reshape(bp, tp, sb_heads, SB_DIM))
        outs[1].append(pp[:, :, 2 * sb_w:3 * sb_w].reshape(bp, tp, sb_heads, SB_DIM))
        outs[2].append(ps[:, :ts, sb_w:2 * sb_w].reshape(bs, ts, sb_heads, SB_DIM))
        outs[3].append(ps[:, :ts, 2 * sb_w:3 * sb_w].reshape(bs, ts, sb_heads, SB_DIM))
        outs[4].append(sp)
        outs[5].append(ss)
        outs[6].append(pp[:, tp - (CONV_W - 1):, 3 * sb_w:3 * sb_w + cw])
        outs[7].append(ps[:, ts - (CONV_W - 1):ts, 3 * sb_w:3 * sb_w + cw])
        outs[8].append(mk)
        outs[9].append(mv)
    return (yp, ys[:, :ts]) + tuple(jnp.stack(o) for o in outs)
```

```python
import functools

import jax
import jax.numpy as jnp
from jax import lax
from jax.experimental import pallas as pl
from jax.experimental.pallas import tpu as pltpu

F32 = jnp.float32
BF16 = jnp.bfloat16

EPS = 1e-6
LOG2E = 1.4426950408889634
SB_DIM = 128
SB_RUN = 256
GDN_DK = 128
GDN_DV = 128
GDN_CHUNK = 64
CONV_W = 4
SUBLANES = 8
LANES = 128
SAMPLE_PAD = 2 * SUBLANES
VMEM_LIMIT = 56 * 1024 * 1024


def _params(*sem):
    return pltpu.CompilerParams(dimension_semantics=sem, vmem_limit_bytes=VMEM_LIMIT)


def _tile(n, pref):
    if n <= pref:
        return n
    t = pref
    while n % t:
        t //= 2
    return t


def _dot(a, b):
    return jnp.dot(a.astype(BF16), b.astype(BF16), preferred_element_type=F32)


def _dot_nt(a, b):
    return lax.dot_general(a.astype(BF16), b.astype(BF16), (((1,), (1,)), ((), ())), preferred_element_type=F32)


def _sigmoid(x):
    return 1.0 / (1.0 + jnp.exp(-x))


def _softplus(x):
    return jnp.maximum(x, 0.0) + jnp.log(1.0 + jnp.exp(-jnp.abs(x)))


def _rmsnorm_body(x_ref, g_ref, o_ref):
    x = x_ref[...].astype(F32)
    ms = jnp.mean(x * x, axis=-1, keepdims=True)
    o_ref[...] = (x * lax.rsqrt(ms + EPS) * g_ref[...]).astype(o_ref.dtype)


def _rmsnorm(x, g, out_dtype=BF16):
    m, d = x.shape
    tm = _tile(m, 512)
    return pl.pallas_call(
        _rmsnorm_body,
        out_shape=jax.ShapeDtypeStruct((m, d), out_dtype),
        grid=(m // tm,),
        in_specs=[pl.BlockSpec((tm, d), lambda i: (i, 0)), pl.BlockSpec((1, d), lambda i: (0, 0))],
        out_specs=pl.BlockSpec((tm, d), lambda i: (i, 0)),
        compiler_params=_params("parallel"),
        name="rmsnorm",
    )(x, g.reshape(1, d).astype(F32))


def _headnorm_body(x_ref, g_ref, o_ref, *, heads, dh):
    for h in range(heads):
        x = x_ref[:, h * dh:(h + 1) * dh].astype(F32)
        ms = jnp.mean(x * x, axis=-1, keepdims=True)
        o_ref[:, h * dh:(h + 1) * dh] = (x * lax.rsqrt(ms + EPS) * g_ref[...]).astype(o_ref.dtype)


def _headnorm(x, g, heads, col_block):
    m = x.shape[0]
    dh = g.shape[-1]
    w = heads * dh
    tm = _tile(m, 512)
    return pl.pallas_call(
        functools.partial(_headnorm_body, heads=heads, dh=dh),
        out_shape=jax.ShapeDtypeStruct((m, w), F32),
        grid=(m // tm,),
        in_specs=[pl.BlockSpec((tm, w), lambda i: (i, col_block)), pl.BlockSpec((1, dh), lambda i: (0, 0))],
        out_specs=pl.BlockSpec((tm, w), lambda i: (i, 0)),
        compiler_params=_params("parallel"),
        name="headnorm",
    )(x, g.reshape(1, dh).astype(F32))


def _mm_body(a_ref, b_ref, o_ref):
    o_ref[...] = jnp.dot(a_ref[...], b_ref[...], preferred_element_type=F32).astype(o_ref.dtype)


def _mm_res_body(a_ref, b_ref, r_ref, o_ref):
    o_ref[...] = (r_ref[...] + jnp.dot(a_ref[...], b_ref[...], preferred_element_type=F32)).astype(o_ref.dtype)


def _matmul(a, b, res=None, out_dtype=F32):
    m, k = a.shape
    n = b.shape[1]
    tm = _tile(m, 1024)
    tn = _tile(n, 512)
    in_specs = [pl.BlockSpec((tm, k), lambda i, j: (i, 0)), pl.BlockSpec((k, tn), lambda i, j: (0, j))]
    args = [a, b]
    body = _mm_body
    if res is not None:
        in_specs.append(pl.BlockSpec((tm, tn), lambda i, j: (i, j)))
        args.append(res)
        body = _mm_res_body
    return pl.pallas_call(
        body,
        out_shape=jax.ShapeDtypeStruct((m, n), out_dtype),
        grid=(m // tm, n // tn),
        in_specs=in_specs,
        out_specs=pl.BlockSpec((tm, tn), lambda i, j: (i, j)),
        compiler_params=_params("parallel", "arbitrary"),
        name="matmul",
    )(*args)


def _sb_runs(items, carries, suffix):
    n = len(items)
    scores, sums, out = {}, {}, {}

    def stack(parts):
        return parts[0] if len(parts) == 1 else jnp.concatenate(parts, axis=0)

    def emit_scores(i):
        _, qs, ks, _, biases, _ = items[i]
        scores[i] = stack([_dot_nt(q, k) + bias for q, k, bias in zip(qs, ks, biases)])

    def emit_sums(i):
        mask = items[i][5]
        z = scores.pop(i)
        neg_abs = lax.bitcast_convert_type(lax.bitcast_convert_type(z, jnp.uint32) | jnp.uint32(0x80000000), F32)
        sp = jnp.maximum(z, 0.0) + jnp.log2(1.0 + jnp.exp2(neg_abs))
        if mask is not None:
            sp = jnp.where(mask, sp, 0.0)
        hi = sp.astype(BF16)
        lo = (sp - hi.astype(F32)).astype(BF16)
        incl = jnp.dot(jnp.concatenate([hi, lo], axis=1), suffix, preferred_element_type=F32)
        sums[i] = (z, incl)

    def emit_values(i):
        group, qs, _, vs, _, mask = items[i]
        z, incl = sums.pop(i)
        carry = carries[group]
        a = jnp.exp2(z - incl - jnp.concatenate([carry] * (z.shape[1] // LANES), axis=1))
        if mask is not None:
            a = jnp.where(mask, a, 0.0)
        carries[group] = carry + jnp.broadcast_to(incl[:, 0:1], carry.shape)
        a = a.astype(BF16)
        r = qs[0].shape[0]
        contrib = stack([jnp.dot(a[j * r:(j + 1) * r], v, preferred_element_type=F32) for j, v in enumerate(vs)])
        out[group] = contrib if group not in out else out[group] + contrib

    for step in range(n + 2):
        if step < n:
            emit_scores(step)
        if 0 <= step - 1 < n:
            emit_sums(step - 1)
        if 0 <= step - 2 < n:
            emit_values(step - 2)
    return out


def _suffix_sum_operand(run):
    j = lax.broadcasted_iota(jnp.int32, (2 * run, run), 0) % run
    s = lax.broadcasted_iota(jnp.int32, (2 * run, run), 1)
    return jnp.where(j >= s, 1.0, 0.0).astype(BF16)


def _sb_prompt_body(bias_ref, q_ref, k_ref, v_ref, suffix_ref, o_ref, kb_ref, vb_ref, qs_ref, acc_ref, carry_ref,
                    *, tq, rb, kb, scale):
    h = pl.program_id(1)
    qi = pl.program_id(2)
    run = suffix_ref.shape[1]

    @pl.when(qi == 0)
    def _():
        kb_ref[...] = k_ref[...].astype(BF16)
        vb_ref[...] = v_ref[...].astype(BF16)

    qs_ref[...] = (q_ref[...] * (scale * LOG2E)).astype(BF16)
    acc_ref[...] = jnp.zeros_like(acc_ref)
    carry_ref[...] = jnp.zeros_like(carry_ref)
    bias = bias_ref[h] * LOG2E
    suffix = suffix_ref[...]
    groups = list(range(tq // rb))

    def update(first_run, n, diagonal):
        items = []
        for r in reversed(range(n)):
            start = pl.multiple_of((first_run + r) * run, run)
            k = kb_ref[pl.ds(start, run), :]
            v = vb_ref[pl.ds(start, run), :]
            for g in groups:
                if diagonal and r > g:
                    continue
                items.append((g, [qs_ref[g * rb:(g + 1) * rb]], [k], [v], [bias],
                              causal if diagonal and r == g else None))
        carries = {g: carry_ref[g * rb:(g + 1) * rb] for g in groups}
        for g, contrib in _sb_runs(items, carries, suffix).items():
            acc_ref[g * rb:(g + 1) * rb] += contrib
            carry_ref[g * rb:(g + 1) * rb] = carries[g]

    causal = lax.broadcasted_iota(jnp.int32, (rb, run), 1) < lax.broadcasted_iota(jnp.int32, (rb, run), 0)
    update(qi * (tq // run), tq // run, True)

    n_trips = (qi * tq) // (kb * run)

    def body(jj, c):
        update((n_trips - 1 - jj) * kb, kb, False)
        return c

    lax.fori_loop(0, n_trips, body, 0)
    o_ref[...] = acc_ref[...].astype(o_ref.dtype)


def _sb_prompt(proj, bias, batch, seq, heads, tq=1024, kb=2):
    d = SB_DIM
    run = SB_RUN
    rb = run
    tq = _tile(seq, tq)
    kb = min(kb, tq // run)
    assert tq % (kb * run) == 0 and tq % rb == 0
    nq = seq // tq
    return pl.pallas_call(
        functools.partial(_sb_prompt_body, tq=tq, rb=rb, kb=kb, scale=d ** -0.5),
        out_shape=jax.ShapeDtypeStruct((batch * seq, heads * d), BF16),
        grid=(batch, heads, nq),
        in_specs=[
            pl.BlockSpec(memory_space=pltpu.SMEM),
            pl.BlockSpec((tq, d), lambda b, h, i: (b * nq + i, h)),
            pl.BlockSpec((seq, d), lambda b, h, i: (b, heads + h)),
            pl.BlockSpec((seq, d), lambda b, h, i: (b, 2 * heads + h)),
            pl.BlockSpec((2 * run, run), lambda b, h, i: (0, 0)),
        ],
        out_specs=pl.BlockSpec((tq, d), lambda b, h, i: (b * nq + i, h)),
        scratch_shapes=[pltpu.VMEM((seq, d), BF16), pltpu.VMEM((seq, d), BF16), pltpu.VMEM((tq, d), BF16),
                        pltpu.VMEM((tq, d), F32), pltpu.VMEM((tq, LANES), F32)],
        compiler_params=_params("parallel", "parallel", "arbitrary"),
        name="sb_prompt",
    )(bias.astype(F32), proj, proj, proj, _suffix_sum_operand(run))


def _sb_sample_body(pt_ref, bias_ref, q_ref, kn_ref, vn_ref, suffix_ref, *refs, heads, pages_per_step, page, scale):
    del pt_ref
    page_refs = refs[:2 * pages_per_step]
    o_ref, acc_ref, carry_ref = refs[2 * pages_per_step:]
    s = pl.program_id(1)
    d = SB_DIM
    run = suffix_ref.shape[1]
    qp = q_ref.shape[0]
    q = (q_ref[...] * (scale * LOG2E)).astype(BF16)
    suffix = suffix_ref[...]
    qs = [q[:, h * d:(h + 1) * d] for h in range(heads)]
    biases = [bias_ref[h] * LOG2E for h in range(heads)]

    @pl.when(s == 0)
    def _():
        pad = jnp.zeros((run - qp, heads * d), F32)
        k_new = jnp.concatenate([kn_ref[...], pad], axis=0).astype(BF16)
        v_new = jnp.concatenate([vn_ref[...], pad], axis=0).astype(BF16)
        row = lax.broadcasted_iota(jnp.int32, (heads * qp, run), 0) % qp
        mask = lax.broadcasted_iota(jnp.int32, (heads * qp, run), 1) < row
        item = (0, qs, [k_new[:, h * d:(h + 1) * d] for h in range(heads)],
                [v_new[:, h * d:(h + 1) * d] for h in range(heads)], biases, mask)
        carries = {0: jnp.zeros((heads * qp, LANES), F32)}
        acc_ref[...] = _sb_runs([item], carries, suffix)[0]
        carry_ref[...] = carries[0]

    def head_rows(ref, h):
        return ref[pl.ds(h, page, stride=heads), :].astype(BF16)

    ppr = run // page
    items = []
    for first in range(0, pages_per_step, ppr):
        order = list(reversed(range(first, first + ppr)))
        ks = [jnp.concatenate([head_rows(page_refs[2 * i], h) for i in order], axis=0) for h in range(heads)]
        vs = [jnp.concatenate([head_rows(page_refs[2 * i + 1], h) for i in order], axis=0) for h in range(heads)]
        items.append((0, qs, ks, vs, biases, None))
    carries = {0: carry_ref[...]}
    acc_ref[...] += _sb_runs(items, carries, suffix)[0]
    carry_ref[...] = carries[0]

    @pl.when(s == pl.num_programs(1) - 1)
    def _():
        for h in range(heads):
            o_ref[:, h * d:(h + 1) * d] = acc_ref[h * qp:(h + 1) * qp, :].astype(o_ref.dtype)


def _sb_sample(proj, bias, cache_k, cache_v, page_table, layer, heads):
    d = SB_DIM
    run = SB_RUN
    batch, n_pages = page_table.shape
    depth, n_phys, page = cache_k.shape[:3]
    w = heads * d
    qp = SAMPLE_PAD
    pps = _tile(n_pages, 16)
    assert (pps * page) % run == 0
    steps = n_pages // pps
    ck = cache_k.reshape(depth * n_phys, page * heads, d)
    cv = cache_v.reshape(depth * n_phys, page * heads, d)

    def page_spec(i):
        def index(b, s, pt):
            return (layer * n_phys + pt[b * n_pages + (n_pages - 1 - (s * pps + i))], 0, 0)
        return pl.BlockSpec((None, page * heads, d), index)

    page_specs = []
    page_args = []
    for i in range(pps):
        page_specs += [page_spec(i), page_spec(i)]
        page_args += [ck, cv]
    grid_spec = pltpu.PrefetchScalarGridSpec(
        num_scalar_prefetch=1,
        grid=(batch, steps),
        in_specs=[
            pl.BlockSpec(memory_space=pltpu.SMEM),
            pl.BlockSpec((qp, w), lambda b, s, pt: (b, 0)),
            pl.BlockSpec((qp, w), lambda b, s, pt: (b, 1)),
            pl.BlockSpec((qp, w), lambda b, s, pt: (b, 2)),
            pl.BlockSpec((2 * run, run), lambda b, s, pt: (0, 0)),
        ] + page_specs,
        out_specs=pl.BlockSpec((qp, w), lambda b, s, pt: (b, 0)),
        scratch_shapes=[pltpu.VMEM((heads * qp, d), F32), pltpu.VMEM((heads * qp, LANES), F32)],
    )
    return pl.pallas_call(
        functools.partial(_sb_sample_body, heads=heads, pages_per_step=pps, page=page, scale=d ** -0.5),
        out_shape=jax.ShapeDtypeStruct((batch * qp, w), BF16),
        grid_spec=grid_spec,
        compiler_params=_params("parallel", "arbitrary"),
        name="sb_sample",
    )(page_table.reshape(-1).astype(jnp.int32), bias.astype(F32), proj, proj, proj,
      _suffix_sum_operand(run), *page_args)


_NN = (((1,), (0,)), ((), ()))
_NT = (((1,), (1,)), ((), ()))
_TN = (((0,), (0,)), ((), ()))


def _split_bf16(x):
    hi = x.astype(BF16)
    return hi, (x - hi.astype(F32)).astype(BF16)


def _mm(a, b, dims, precise):
    dot = lambda x, y: lax.dot_general(x, y, dims, preferred_element_type=F32)
    if not precise:
        return dot(a.astype(BF16), b.astype(BF16))
    ah, al = _split_bf16(a)
    bh, bl = _split_bf16(b)
    return dot(ah, bh) + dot(ah, bl) + dot(al, bh)


def _split3_bf16(x):
    hi = x.astype(BF16)
    rest = x - hi.astype(F32)
    mid = rest.astype(BF16)
    return hi, mid, (rest - mid.astype(F32)).astype(BF16)


def _mm_exact_lhs(a, b):
    a = a.astype(BF16)
    return sum(jnp.dot(a, piece, preferred_element_type=F32) for piece in _split3_bf16(b))


def _mm_exact_rhs(a, b):
    b = b.astype(BF16)
    return sum(jnp.dot(piece, b, preferred_element_type=F32) for piece in _split3_bf16(a))


def _unit_lower_inverses(ms):
    c = ms[0].shape[0]
    eye = (lax.broadcasted_iota(jnp.int32, (c, c), 0) == lax.broadcasted_iota(jnp.int32, (c, c), 1)).astype(F32)
    ps = [-m for m in ms]
    invs = [eye + p for p in ps]
    span = 2
    while span < c:
        ps = [_mm(p, p, _NN, True) for p in ps]
        invs = [inv + _mm(inv, p, _NN, True) for inv, p in zip(invs, ps)]
        span *= 2
    return invs


def _gdn_body(qkv_ref, z_ref, ba_ref, bat_ref, prev_ref, s0_ref, wconv_ref, arow_ref, dtrow_ref,
              acol_ref, dtcol_ref, gnorm_ref, o_ref, s_ref, tail_ref, *, heads, chunk, n_valid, precise):
    t = pl.program_id(1)
    tt = qkv_ref.shape[0]
    kw = heads * GDN_DK
    sub = SUBLANES

    @pl.when(t == 0)
    def _():
        tail_ref[...] = prev_ref[...]
        s_ref[...] = s0_ref[...]

    x = qkv_ref[...]
    head_rows = jnp.concatenate([tail_ref[...], x[:sub]], axis=0)
    y = x * wconv_ref[CONV_W - 1:CONV_W, :]
    for shift in range(1, CONV_W):
        first = pltpu.roll(head_rows, shift, 0)[sub:]
        if tt > sub:
            shifted = jnp.concatenate([first, pltpu.roll(x, shift, 0)[sub:]], axis=0)
        else:
            shifted = first
        y = y + shifted * wconv_ref[CONV_W - 1 - shift:CONV_W - shift, :]
    tail_ref[...] = x[tt - sub:]
    y = y * _sigmoid(y)

    n_chunks = max(tt, chunk) // chunk

    def rows(a, c):
        if tt < chunk:
            return jnp.concatenate([a, jnp.zeros((chunk - tt, a.shape[1]), a.dtype)], axis=0)
        return a[c * chunk:(c + 1) * chunk]

    ri = lax.broadcasted_iota(jnp.int32, (chunk, chunk), 0)
    ci = lax.broadcasted_iota(jnp.int32, (chunk, chunk), 1)
    incl = ri >= ci
    strict = ri > ci
    tri = incl.astype(F32)
    tri_t = (ri <= ci).astype(F32)

    items = []
    for c in range(n_chunks):
        ba = rows(ba_ref[...], c)
        g_tok = -jnp.exp(arow_ref[...]) * _softplus(ba + dtrow_ref[...])
        beta_tok = _sigmoid(ba)
        bat = bat_ref[:, c * chunk:(c + 1) * chunk]
        g_row = -jnp.exp(acol_ref[...]) * _softplus(bat + dtcol_ref[...])
        if n_valid is not None:
            valid_r = lax.broadcasted_iota(jnp.int32, g_tok.shape, 0) < n_valid
            g_tok = jnp.where(valid_r, g_tok, 0.0)
            beta_tok = jnp.where(valid_r, beta_tok, 0.0)
            g_row = jnp.where(lax.broadcasted_iota(jnp.int32, g_row.shape, 1) < n_valid, g_row, 0.0)
        gc_tok = _mm_exact_lhs(tri, g_tok)
        gc_row = _mm_exact_rhs(g_row, tri_t)
        eg_tok = jnp.exp(gc_tok)
        yc = rows(y, c)
        for h in range(heads):
            gcol = gc_tok[:, heads + h:heads + h + 1]
            grow = gc_row[heads + h:heads + h + 1, :]
            beta = beta_tok[:, h:h + 1]
            eg = eg_tok[:, heads + h:heads + h + 1]
            q = yc[:, h * GDN_DK:(h + 1) * GDN_DK]
            k = yc[:, kw + h * GDN_DK:kw + (h + 1) * GDN_DK]
            v = yc[:, 2 * kw + h * GDN_DV:2 * kw + (h + 1) * GDN_DV]
            q = q * lax.rsqrt(jnp.sum(q * q, axis=-1, keepdims=True) + EPS) * (GDN_DK ** -0.5)
            k = k * lax.rsqrt(jnp.sum(k * k, axis=-1, keepdims=True) + EPS)
            g_last = gcol[chunk - 1:chunk, :]
            items.append(dict(
                c=c, h=h, q=q, k=k, kb=k * beta, g_last=g_last,
                decay=jnp.where(incl, jnp.exp(jnp.where(incl, gcol - grow, 0.0)), 0.0),
                rhs=jnp.concatenate([v * beta, (k * beta) * eg], axis=1),
                q_dec=q * eg, k_dec=k * jnp.exp(g_last - gcol)))
    kks = [_mm(it["kb"], it["k"], _NT, precise) for it in items]
    qks = [_mm(it["q"], it["k"], _NT, precise) for it in items]
    invs = _unit_lower_inverses([jnp.where(strict, kk * it["decay"], 0.0) for kk, it in zip(kks, items)])
    uws = [_mm(inv, it["rhs"], _NN, True) for inv, it in zip(invs, items)]
    qks = [jnp.where(incl, qk * it["decay"], 0.0) for qk, it in zip(qks, items)]

    for c in range(n_chunks):
        sel = [i for i, it in enumerate(items) if it["c"] == c]
        states = [s_ref[items[i]["h"]] for i in sel]
        ws = [_mm(jnp.concatenate([uws[i][:, GDN_DV:], items[i]["q_dec"]], axis=0), s, _NN, precise)
              for i, s in zip(sel, states)]
        v_news = [uws[i][:, :GDN_DV] - w[:chunk] for i, w in zip(sel, ws)]
        outs = [w[chunk:] + _mm(qks[i], vn, _NN, precise) for i, w, vn in zip(sel, ws, v_news)]
        deltas = [_mm(items[i]["k_dec"], vn, _TN, precise) for i, vn in zip(sel, v_news)]
        zc = rows(z_ref[...], c)
        for i, s, o, delta in zip(sel, states, outs, deltas):
            h = items[i]["h"]
            s_ref[h] = s * jnp.exp(items[i]["g_last"]) + delta
            o = o * lax.rsqrt(jnp.mean(o * o, axis=-1, keepdims=True) + EPS) * gnorm_ref[...]
            zz = zc[:, h * GDN_DV:(h + 1) * GDN_DV]
            out = (o * (zz * _sigmoid(zz))).astype(o_ref.dtype)
            if tt < chunk:
                o_ref[:, h * GDN_DV:(h + 1) * GDN_DV] = out[:tt]
            else:
                o_ref[c * chunk:(c + 1) * chunk, h * GDN_DV:(h + 1) * GDN_DV] = out


def _gdn(proj, prev, s0, w_conv, a_log, dt_bias, gdn_norm, batch, seq, heads, qkv_block, z_block, ba_block,
         n_valid, precise):
    kw = heads * GDN_DK
    vw = heads * GDN_DV
    cw = 2 * kw + vw
    chunk = GDN_CHUNK
    tt = _tile(seq, 128)
    assert tt % chunk == 0 or tt < chunk
    nt = seq // tt
    ba = proj[:, ba_block * LANES:ba_block * LANES + 2 * SUBLANES]
    bat = jnp.swapaxes(ba.reshape(batch, seq, 2 * SUBLANES), 1, 2)
    if seq < chunk:
        bat = jnp.pad(bat, ((0, 0), (0, 0), (0, chunk - seq)))
    arow = jnp.zeros((1, LANES), F32).at[0, heads:2 * heads].set(a_log.astype(F32))
    dtrow = jnp.zeros((1, LANES), F32).at[0, heads:2 * heads].set(dt_bias.astype(F32))
    acol = arow[0, :2 * SUBLANES].reshape(2 * SUBLANES, 1)
    dtcol = dtrow[0, :2 * SUBLANES].reshape(2 * SUBLANES, 1)
    const = lambda b, t: (0, 0)
    o, s_new = pl.pallas_call(
        functools.partial(_gdn_body, heads=heads, chunk=chunk, n_valid=n_valid, precise=precise),
        out_shape=(jax.ShapeDtypeStruct((batch * seq, vw), BF16),
                   jax.ShapeDtypeStruct((batch, heads, GDN_DK, GDN_DV), F32)),
        grid=(batch, nt),
        in_specs=[
            pl.BlockSpec((tt, cw), lambda b, t: (b * nt + t, qkv_block)),
            pl.BlockSpec((tt, vw), lambda b, t: (b * nt + t, z_block)),
            pl.BlockSpec((tt, LANES), lambda b, t: (b * nt + t, ba_block)),
            pl.BlockSpec((None, 2 * SUBLANES, max(tt, chunk)), lambda b, t: (b, 0, t)),
            pl.BlockSpec((None, SUBLANES, cw), lambda b, t: (b, 0, 0)),
            pl.BlockSpec((None, heads, GDN_DK, GDN_DV), lambda b, t: (b, 0, 0, 0)),
            pl.BlockSpec((CONV_W, cw), const),
            pl.BlockSpec((1, LANES), const),
            pl.BlockSpec((1, LANES), const),
            pl.BlockSpec((2 * SUBLANES, 1), const),
            pl.BlockSpec((2 * SUBLANES, 1), const),
            pl.BlockSpec((1, GDN_DV), const),
        ],
        out_specs=(pl.BlockSpec((tt, vw), lambda b, t: (b * nt + t, 0)),
                   pl.BlockSpec((None, heads, GDN_DK, GDN_DV), lambda b, t: (b, 0, 0, 0))),
        scratch_shapes=[pltpu.VMEM((SUBLANES, cw), F32)],
        compiler_params=_params("parallel", "arbitrary"),
        name="gdn",
    )(proj, proj, proj, bat, prev, s0, w_conv.astype(F32), arow, dtrow, acol, dtcol,
      gdn_norm.reshape(1, GDN_DV).astype(F32))
    return o, s_new


def _mem_attn_body(q_ref, k_ref, v_ref, g_ref, o_ref, *, heads, dh):
    scale = dh ** -0.5
    for h in range(heads):
        hs = slice(h * dh, (h + 1) * dh)
        q = q_ref[:, hs]
        q = q * lax.rsqrt(jnp.mean(q * q, axis=-1, keepdims=True) + EPS) * g_ref[...]
        s = _dot_nt(q, k_ref[:, hs]) * scale
        e = jnp.exp(s - jnp.max(s, axis=-1, keepdims=True))
        p = e / jnp.sum(e, axis=-1, keepdims=True)
        o_ref[:, hs] = _dot(p, v_ref[:, hs]).astype(o_ref.dtype)


def _mem_attn(proj, mem_k, mem_v, q_norm, batch, seq, q_block):
    n_mem, heads, dh = mem_k.shape[1:]
    w = heads * dh
    tt = _tile(seq, 512)
    nt = seq // tt
    return pl.pallas_call(
        functools.partial(_mem_attn_body, heads=heads, dh=dh),
        out_shape=jax.ShapeDtypeStruct((batch * seq, w), BF16),
        grid=(batch, nt),
        in_specs=[
            pl.BlockSpec((tt, w), lambda b, t: (b * nt + t, q_block)),
            pl.BlockSpec((None, n_mem, w), lambda b, t: (b, 0, 0)),
            pl.BlockSpec((None, n_mem, w), lambda b, t: (b, 0, 0)),
            pl.BlockSpec((1, dh), lambda b, t: (0, 0)),
        ],
        out_specs=pl.BlockSpec((tt, w), lambda b, t: (b * nt + t, 0)),
        compiler_params=_params("parallel", "parallel"),
        name="mem_attn",
    )(proj, mem_k.reshape(batch, n_mem, w), mem_v.reshape(batch, n_mem, w), q_norm.reshape(1, dh).astype(F32))


def _merge_body(a0_ref, a1_ref, a2_ref, w0_ref, w1_ref, w2_ref, g0_ref, g1_ref, g2_ref, o_ref):
    acc = _sigmoid(g0_ref[...]) * jnp.dot(a0_ref[...], w0_ref[...], preferred_element_type=F32)
    acc += _sigmoid(g1_ref[...]) * jnp.dot(a1_ref[...], w1_ref[...], preferred_element_type=F32)
    acc += _sigmoid(g2_ref[...]) * jnp.dot(a2_ref[...], w2_ref[...], preferred_element_type=F32)
    o_ref[...] = acc.astype(o_ref.dtype)


def _merge(branches, weights, proj, gate_col, d_model):
    m = proj.shape[0]
    tm = _tile(m, 512)
    tn = _tile(d_model, 512)
    a_specs = [pl.BlockSpec((tm, a.shape[1]), lambda i, j: (i, 0)) for a in branches]
    w_specs = [pl.BlockSpec((w.shape[0], tn), lambda i, j: (0, j)) for w in weights]
    g_specs = [pl.BlockSpec((tm, tn), functools.partial(lambda i, j, off: (i, off + j), off=(gate_col + g * d_model) // tn))
               for g in range(3)]
    return pl.pallas_call(
        _merge_body,
        out_shape=jax.ShapeDtypeStruct((m, d_model), BF16),
        grid=(m // tm, d_model // tn),
        in_specs=a_specs + w_specs + g_specs,
        out_specs=pl.BlockSpec((tm, tn), lambda i, j: (i, j)),
        compiler_params=_params("parallel", "arbitrary"),
        name="merge",
    )(*branches, *weights, proj, proj, proj)


def _ffn_body(x_ref, g_ref, wg_ref, wu_ref, wo_ref, o_ref, h_ref, *, splits):
    f = pl.program_id(1)

    @pl.when(f == 0)
    def _():
        x = x_ref[...]
        ms = jnp.mean(x * x, axis=-1, keepdims=True)
        h_ref[...] = (x * lax.rsqrt(ms + EPS) * g_ref[...]).astype(BF16)
        o_ref[...] = x

    h = h_ref[...]
    gate = jnp.dot(h, wg_ref[...], preferred_element_type=F32)
    up = jnp.dot(h, wu_ref[...], preferred_element_type=F32)
    act = (gate * _sigmoid(gate) * up).astype(BF16)
    w = o_ref.shape[1] // splits
    for s in range(splits):
        o_ref[:, s * w:(s + 1) * w] += jnp.dot(act, wo_ref[:, s * w:(s + 1) * w], preferred_element_type=F32)


def _ffn(x, g, w_in, w_out, splits=4):
    m, d = x.shape
    dff = w_out.shape[0]
    tm = _tile(m, 512)
    tf = _tile(dff, 512)
    nf = dff // tf
    return pl.pallas_call(
        functools.partial(_ffn_body, splits=splits),
        out_shape=jax.ShapeDtypeStruct((m, d), F32),
        grid=(m // tm, nf),
        in_specs=[
            pl.BlockSpec((tm, d), lambda i, f: (i, 0)),
            pl.BlockSpec((1, d), lambda i, f: (0, 0)),
            pl.BlockSpec((d, tf), lambda i, f: (0, f)),
            pl.BlockSpec((d, tf), lambda i, f: (0, nf + f)),
            pl.BlockSpec((tf, d), lambda i, f: (f, 0)),
        ],
        out_specs=pl.BlockSpec((tm, d), lambda i, f: (i, 0)),
        scratch_shapes=[pltpu.VMEM((tm, d), BF16)],
        compiler_params=_params("parallel", "arbitrary"),
        name="ffn",
    )(x, g.reshape(1, d).astype(F32), w_in, w_in, w_out)


def _trunk(x, p, sb_attend, conv_prev, s0, mem_k, mem_v, n_valid, precise_gdn):
    batch, seq, d = x.shape
    heads = p["sb_heads"]
    x2 = x.reshape(batch * seq, d)
    h = _rmsnorm(x2, p["ln_mix"])
    proj = _matmul(h, p["w_main"])
    o_sb = sb_attend(proj)
    o_gdn, s_new = _gdn(proj, conv_prev, s0, p["w_conv"], p["a_log"], p["dt_bias"], p["gdn_norm"],
                        batch, seq, p["gdn_heads"], qkv_block=1, z_block=p["z_block"], ba_block=p["ba_block"],
                        n_valid=n_valid, precise=precise_gdn)
    o_mem = _mem_attn(proj, mem_k, mem_v, p["q_norm_mem"], batch, seq, q_block=p["memq_block"])
    merged = _merge([o_sb, o_gdn, o_mem], [p["w_proj_sb"], p["w_proj_gdn"], p["w_proj_mem"]],
                    proj, p["gate_col"], d)
    x1 = _matmul(merged, p["w_out"], res=x2)
    y = _ffn(x1, p["ln_ffn"], p["w_ffn_in"], p["w_ffn_out"])
    return y.reshape(batch, seq, d), proj, s_new


def kernel(x_prompt, x_sample, mem_prompt, cache_sb_k, cache_sb_v, state_gdn, state_conv, cache_mem_k, cache_mem_v, page_table, ln_mix, w_in, sb_bias, w_conv, a_log, dt_bias, gdn_norm, q_norm_mem, k_norm_mem, ln_mem, w_mem_kv, w_proj_sb, w_proj_gdn, w_proj_mem, w_out, ln_ffn, w_ffn_in, w_ffn_out):
    depth = w_in.shape[0]
    d_model = x_prompt.shape[-1]
    bp, tp = x_prompt.shape[:2]
    bs, ts = x_sample.shape[:2]
    sb_heads = sb_bias.shape[-1]
    gdn_heads = a_log.shape[-1]
    mem_heads, mem_dim = cache_mem_k.shape[-2:]
    n_mem = mem_prompt.shape[1]
    sb_w = sb_heads * SB_DIM
    kw = gdn_heads * GDN_DK
    vw = gdn_heads * GDN_DV
    cw = 2 * kw + vw
    mem_w = mem_heads * mem_dim
    assert sb_w == kw == vw == mem_w and cw == 3 * sb_w and 2 * gdn_heads <= 2 * SUBLANES
    off_ba = 3 * sb_w + cw + vw
    off_memq = off_ba + 2 * gdn_heads
    main_w = off_ba + mem_w + 3 * d_model
    assert w_in.shape[-1] == off_memq + mem_w + 3 * d_model and main_w % LANES == 0
    proj_tn = 512
    proj_w = -(-(main_w + LANES) // proj_tn) * proj_tn

    yp = x_prompt
    ys = jnp.pad(x_sample, ((0, 0), (0, SAMPLE_PAD - ts), (0, 0)))
    outs = [[] for _ in range(10)]
    for l in range(depth):
        w_main = jnp.concatenate(
            [w_in[l][:, :off_ba], w_in[l][:, off_memq:], w_in[l][:, off_ba:off_memq],
             jnp.zeros((d_model, proj_w - main_w - 2 * gdn_heads), w_in.dtype)], axis=1).astype(BF16)
        p = {
            "sb_heads": sb_heads, "gdn_heads": gdn_heads, "ba_block": main_w // LANES,
            "z_block": (3 * sb_w + cw) // vw, "memq_block": off_ba // mem_w, "gate_col": off_ba + mem_w,
            "ln_mix": ln_mix[l], "w_main": w_main, "w_conv": w_conv[l], "a_log": a_log[l],
            "dt_bias": dt_bias[l], "gdn_norm": gdn_norm[l], "q_norm_mem": q_norm_mem[l],
            "w_proj_sb": w_proj_sb[l].astype(BF16), "w_proj_gdn": w_proj_gdn[l].astype(BF16),
            "w_proj_mem": w_proj_mem[l].astype(BF16), "w_out": w_out[l].astype(BF16), "ln_ffn": ln_ffn[l],
            "w_ffn_in": w_ffn_in[l].astype(BF16), "w_ffn_out": w_ffn_out[l].astype(BF16),
        }

        kv = _matmul(_rmsnorm(mem_prompt.reshape(bp * n_mem, d_model), ln_mem[l]), w_mem_kv[l].astype(BF16))
        mk = _headnorm(kv, k_norm_mem[l], mem_heads, 0).reshape(bp, n_mem, mem_heads, mem_dim)
        mv = kv[:, mem_w:].reshape(bp, n_mem, mem_heads, mem_dim)
        yp, proj_p, sp = _trunk(
            yp, p, lambda proj: _sb_prompt(proj, sb_bias[l], bp, tp, sb_heads),
            jnp.zeros((bp, SUBLANES, cw), F32), jnp.zeros((bp, gdn_heads, GDN_DK, GDN_DV), F32),
            mk, mv, n_valid=None, precise_gdn=False)

        conv_prev = jnp.pad(state_conv[l], ((0, 0), (SUBLANES - (CONV_W - 1), 0), (0, 0)))
        ys, proj_s, ss = _trunk(
            ys, p, lambda proj: _sb_sample(proj, sb_bias[l], cache_sb_k, cache_sb_v, page_table, l, sb_heads),
            conv_prev, state_gdn[l], cache_mem_k[l], cache_mem_v[l], n_valid=ts, precise_gdn=True)

        pp = proj_p.reshape(bp, tp, proj_w)
        ps = proj_s.reshape(bs, SAMPLE_PAD, proj_w)
        outs[0].append(pp[:, :, sb_w:2 * sb_w].reshape(bp, tp, sb_heads, SB_DIM))
        outs[1].append(pp[:, :, 2 * sb_w:3 * sb_w].reshape(bp, tp, sb_heads, SB_DIM))
        outs[2].append(ps[:, :ts, sb_w:2 * sb_w].reshape(bs, ts, sb_heads, SB_DIM))
        outs[3].append(ps[:, :ts, 2 * sb_w:3 * sb_w].reshape(bs, ts, sb_heads, SB_DIM))
        outs[4].append(sp)
        outs[5].append(ss)
        outs[6].append(pp[:, tp - (CONV_W - 1):, 3 * sb_w:3 * sb_w + cw])
        outs[7].append(ps[:, ts - (CONV_W - 1):ts, 3 * sb_w:3 * sb_w + cw])
        outs[8].append(mk)
        outs[9].append(mv)
    return (yp, ys[:, :ts]) + tuple(jnp.stack(o) for o in outs)
```

```python
import functools

import jax
import jax.numpy as jnp
from jax import lax
from jax.experimental import pallas as pl
from jax.experimental.pallas import tpu as pltpu

F32 = jnp.float32
BF16 = jnp.bfloat16

EPS = 1e-6
LOG2E = 1.4426950408889634
SB_DIM = 128
SB_RUN = 256
GDN_DK = 128
GDN_DV = 128
GDN_CHUNK = 64
CONV_W = 4
SUBLANES = 8
LANES = 128
SAMPLE_PAD = 2 * SUBLANES
VMEM_LIMIT = 56 * 1024 * 1024


def _params(*sem):
    return pltpu.CompilerParams(dimension_semantics=sem, vmem_limit_bytes=VMEM_LIMIT)


def _tile(n, pref):
    if n <= pref:
        return n
    t = pref
    while n % t:
        t //= 2
    return t


def _dot(a, b):
    return jnp.dot(a.astype(BF16), b.astype(BF16), preferred_element_type=F32)


def _dot_nt(a, b):
    return lax.dot_general(a.astype(BF16), b.astype(BF16), (((1,), (1,)), ((), ())), preferred_element_type=F32)


def _sigmoid(x):
    return 1.0 / (1.0 + jnp.exp(-x))


def _softplus(x):
    return jnp.maximum(x, 0.0) + jnp.log(1.0 + jnp.exp(-jnp.abs(x)))


def _rmsnorm_body(x_ref, g_ref, o_ref):
    x = x_ref[...].astype(F32)
    ms = jnp.mean(x * x, axis=-1, keepdims=True)
    o_ref[...] = (x * lax.rsqrt(ms + EPS) * g_ref[...]).astype(o_ref.dtype)


def _rmsnorm(x, g, out_dtype=BF16):
    m, d = x.shape
    tm = _tile(m, 512)
    return pl.pallas_call(
        _rmsnorm_body,
        out_shape=jax.ShapeDtypeStruct((m, d), out_dtype),
        grid=(m // tm,),
        in_specs=[pl.BlockSpec((tm, d), lambda i: (i, 0)), pl.BlockSpec((1, d), lambda i: (0, 0))],
        out_specs=pl.BlockSpec((tm, d), lambda i: (i, 0)),
        compiler_params=_params("parallel"),
        name="rmsnorm",
    )(x, g.reshape(1, d).astype(F32))


def _headnorm_body(x_ref, g_ref, o_ref, *, heads, dh):
    for h in range(heads):
        x = x_ref[:, h * dh:(h + 1) * dh].astype(F32)
        ms = jnp.mean(x * x, axis=-1, keepdims=True)
        o_ref[:, h * dh:(h + 1) * dh] = (x * lax.rsqrt(ms + EPS) * g_ref[...]).astype(o_ref.dtype)


def _headnorm(x, g, heads, col_block):
    m = x.shape[0]
    dh = g.shape[-1]
    w = heads * dh
    tm = _tile(m, 512)
    return pl.pallas_call(
        functools.partial(_headnorm_body, heads=heads, dh=dh),
        out_shape=jax.ShapeDtypeStruct((m, w), F32),
        grid=(m // tm,),
        in_specs=[pl.BlockSpec((tm, w), lambda i: (i, col_block)), pl.BlockSpec((1, dh), lambda i: (0, 0))],
        out_specs=pl.BlockSpec((tm, w), lambda i: (i, 0)),
        compiler_params=_params("parallel"),
        name="headnorm",
    )(x, g.reshape(1, dh).astype(F32))


def _mm_body(a_ref, b_ref, o_ref):
    o_ref[...] = jnp.dot(a_ref[...], b_ref[...], preferred_element_type=F32).astype(o_ref.dtype)


def _mm_res_body(a_ref, b_ref, r_ref, o_ref):
    o_ref[...] = (r_ref[...] + jnp.dot(a_ref[...], b_ref[...], preferred_element_type=F32)).astype(o_ref.dtype)


def _mm_side_body(a_ref, b_ref, s_ref, o_ref, so_ref):
    o_ref[...] = jnp.dot(a_ref[...], b_ref[...], preferred_element_type=F32)

    @pl.when(pl.program_id(1) == 0)
    def _():
        so_ref[...] = jnp.dot(a_ref[...], s_ref[...], preferred_element_type=F32)


def _matmul_with_side(a, b, side):
    m, k = a.shape
    n = b.shape[1]
    ns = side.shape[1]
    tm = _tile(m, 1024)
    tn = _tile(n, 512)
    return pl.pallas_call(
        _mm_side_body,
        out_shape=(jax.ShapeDtypeStruct((m, n), F32), jax.ShapeDtypeStruct((m, ns), F32)),
        grid=(m // tm, n // tn),
        in_specs=[pl.BlockSpec((tm, k), lambda i, j: (i, 0)), pl.BlockSpec((k, tn), lambda i, j: (0, j)),
                  pl.BlockSpec((k, ns), lambda i, j: (0, 0))],
        out_specs=(pl.BlockSpec((tm, tn), lambda i, j: (i, j)), pl.BlockSpec((tm, ns), lambda i, j: (i, 0))),
        compiler_params=_params("parallel", "arbitrary"),
        name="matmul_side",
    )(a, b, side)


def _matmul(a, b, res=None, out_dtype=F32):
    m, k = a.shape
    n = b.shape[1]
    tm = _tile(m, 1024)
    tn = _tile(n, 512)
    in_specs = [pl.BlockSpec((tm, k), lambda i, j: (i, 0)), pl.BlockSpec((k, tn), lambda i, j: (0, j))]
    args = [a, b]
    body = _mm_body
    if res is not None:
        in_specs.append(pl.BlockSpec((tm, tn), lambda i, j: (i, j)))
        args.append(res)
        body = _mm_res_body
    return pl.pallas_call(
        body,
        out_shape=jax.ShapeDtypeStruct((m, n), out_dtype),
        grid=(m // tm, n // tn),
        in_specs=in_specs,
        out_specs=pl.BlockSpec((tm, tn), lambda i, j: (i, j)),
        compiler_params=_params("parallel", "arbitrary"),
        name="matmul",
    )(*args)


def _sb_runs(items, carries, suffix):
    n = len(items)
    scores, sums, out = {}, {}, {}

    def stack(parts):
        return parts[0] if len(parts) == 1 else jnp.concatenate(parts, axis=0)

    def emit_scores(i):
        _, qs, ks, _, biases, _ = items[i]
        scores[i] = stack([_dot_nt(q, k) + bias for q, k, bias in zip(qs, ks, biases)])

    def emit_sums(i):
        mask = items[i][5]
        z = scores.pop(i)
        neg_abs = lax.bitcast_convert_type(lax.bitcast_convert_type(z, jnp.uint32) | jnp.uint32(0x80000000), F32)
        sp = jnp.maximum(z, 0.0) + jnp.log2(1.0 + jnp.exp2(neg_abs))
        if mask is not None:
            sp = jnp.where(mask, sp, 0.0)
        hi = sp.astype(BF16)
        lo = (sp - hi.astype(F32)).astype(BF16)
        incl = jnp.dot(jnp.concatenate([hi, lo], axis=1), suffix, preferred_element_type=F32)
        sums[i] = (z, incl)

    def emit_values(i):
        group, qs, _, vs, _, mask = items[i]
        z, incl = sums.pop(i)
        carry = carries[group]
        a = jnp.exp2(z - incl - jnp.concatenate([carry] * (z.shape[1] // LANES), axis=1))
        if mask is not None:
            a = jnp.where(mask, a, 0.0)
        carries[group] = carry + jnp.broadcast_to(incl[:, 0:1], carry.shape)
        a = a.astype(BF16)
        r = qs[0].shape[0]
        contrib = stack([jnp.dot(a[j * r:(j + 1) * r], v, preferred_element_type=F32) for j, v in enumerate(vs)])
        out[group] = contrib if group not in out else out[group] + contrib

    for step in range(n + 2):
        if step < n:
            emit_scores(step)
        if 0 <= step - 1 < n:
            emit_sums(step - 1)
        if 0 <= step - 2 < n:
            emit_values(step - 2)
    return out


def _suffix_sum_operand(run):
    j = lax.broadcasted_iota(jnp.int32, (2 * run, run), 0) % run
    s = lax.broadcasted_iota(jnp.int32, (2 * run, run), 1)
    return jnp.where(j >= s, 1.0, 0.0).astype(BF16)


def _sb_prompt_body(bias_ref, q_ref, k_ref, v_ref, suffix_ref, o_ref, kb_ref, vb_ref, qs_ref, acc_ref, carry_ref,
                    *, tq, rb, kb, scale):
    h = pl.program_id(1)
    qi = pl.program_id(2)
    run = suffix_ref.shape[1]

    @pl.when(qi == 0)
    def _():
        kb_ref[...] = k_ref[...].astype(BF16)
        vb_ref[...] = v_ref[...].astype(BF16)

    qs_ref[...] = (q_ref[...] * (scale * LOG2E)).astype(BF16)
    acc_ref[...] = jnp.zeros_like(acc_ref)
    carry_ref[...] = jnp.zeros_like(carry_ref)
    bias = bias_ref[h] * LOG2E
    suffix = suffix_ref[...]
    groups = list(range(tq // rb))

    def update(first_run, n, diagonal):
        items = []
        for r in reversed(range(n)):
            start = pl.multiple_of((first_run + r) * run, run)
            k = kb_ref[pl.ds(start, run), :]
            v = vb_ref[pl.ds(start, run), :]
            for g in groups:
                if diagonal and r > g:
                    continue
                items.append((g, [qs_ref[g * rb:(g + 1) * rb]], [k], [v], [bias],
                              causal if diagonal and r == g else None))
        carries = {g: carry_ref[g * rb:(g + 1) * rb] for g in groups}
        for g, contrib in _sb_runs(items, carries, suffix).items():
            acc_ref[g * rb:(g + 1) * rb] += contrib
            carry_ref[g * rb:(g + 1) * rb] = carries[g]

    causal = lax.broadcasted_iota(jnp.int32, (rb, run), 1) < lax.broadcasted_iota(jnp.int32, (rb, run), 0)
    update(qi * (tq // run), tq // run, True)

    n_trips = (qi * tq) // (kb * run)

    def body(jj, c):
        update((n_trips - 1 - jj) * kb, kb, False)
        return c

    lax.fori_loop(0, n_trips, body, 0)
    o_ref[...] = acc_ref[...].astype(o_ref.dtype)


def _sb_prompt(proj, bias, batch, seq, heads, tq=1024, kb=4):
    d = SB_DIM
    run = SB_RUN
    rb = run
    tq = _tile(seq, tq)
    kb = min(kb, tq // run)
    assert tq % (kb * run) == 0 and tq % rb == 0
    nq = seq // tq
    return pl.pallas_call(
        functools.partial(_sb_prompt_body, tq=tq, rb=rb, kb=kb, scale=d ** -0.5),
        out_shape=jax.ShapeDtypeStruct((batch * seq, heads * d), BF16),
        grid=(batch, heads, nq),
        in_specs=[
            pl.BlockSpec(memory_space=pltpu.SMEM),
            pl.BlockSpec((tq, d), lambda b, h, i: (b * nq + i, h)),
            pl.BlockSpec((seq, d), lambda b, h, i: (b, heads + h)),
            pl.BlockSpec((seq, d), lambda b, h, i: (b, 2 * heads + h)),
            pl.BlockSpec((2 * run, run), lambda b, h, i: (0, 0)),
        ],
        out_specs=pl.BlockSpec((tq, d), lambda b, h, i: (b * nq + i, h)),
        scratch_shapes=[pltpu.VMEM((seq, d), BF16), pltpu.VMEM((seq, d), BF16), pltpu.VMEM((tq, d), BF16),
                        pltpu.VMEM((tq, d), F32), pltpu.VMEM((tq, LANES), F32)],
        compiler_params=_params("parallel", "parallel", "arbitrary"),
        name="sb_prompt",
    )(bias.astype(F32), proj, proj, proj, _suffix_sum_operand(run))


def _sb_sample_body(pt_ref, bias_ref, q_ref, kn_ref, vn_ref, suffix_ref, *refs, heads, pages_per_step, page, scale):
    del pt_ref
    page_refs = refs[:2 * pages_per_step]
    o_ref, acc_ref, carry_ref = refs[2 * pages_per_step:]
    s = pl.program_id(1)
    d = SB_DIM
    run = suffix_ref.shape[1]
    qp = q_ref.shape[0]
    q = (q_ref[...] * (scale * LOG2E)).astype(BF16)
    suffix = suffix_ref[...]
    qs = [q[:, h * d:(h + 1) * d] for h in range(heads)]
    biases = [bias_ref[h] * LOG2E for h in range(heads)]

    @pl.when(s == 0)
    def _():
        pad = jnp.zeros((run - qp, heads * d), F32)
        k_new = jnp.concatenate([kn_ref[...], pad], axis=0).astype(BF16)
        v_new = jnp.concatenate([vn_ref[...], pad], axis=0).astype(BF16)
        row = lax.broadcasted_iota(jnp.int32, (heads * qp, run), 0) % qp
        mask = lax.broadcasted_iota(jnp.int32, (heads * qp, run), 1) < row
        item = (0, qs, [k_new[:, h * d:(h + 1) * d] for h in range(heads)],
                [v_new[:, h * d:(h + 1) * d] for h in range(heads)], biases, mask)
        carries = {0: jnp.zeros((heads * qp, LANES), F32)}
        acc_ref[...] = _sb_runs([item], carries, suffix)[0]
        carry_ref[...] = carries[0]

    def head_rows(ref, h):
        return ref[pl.ds(h, page, stride=heads), :].astype(BF16)

    ppr = run // page
    items = []
    for first in range(0, pages_per_step, ppr):
        order = list(reversed(range(first, first + ppr)))
        ks = [jnp.concatenate([head_rows(page_refs[2 * i], h) for i in order], axis=0) for h in range(heads)]
        vs = [jnp.concatenate([head_rows(page_refs[2 * i + 1], h) for i in order], axis=0) for h in range(heads)]
        items.append((0, qs, ks, vs, biases, None))
    carries = {0: carry_ref[...]}
    acc_ref[...] += _sb_runs(items, carries, suffix)[0]
    carry_ref[...] = carries[0]

    @pl.when(s == pl.num_programs(1) - 1)
    def _():
        for h in range(heads):
            o_ref[:, h * d:(h + 1) * d] = acc_ref[h * qp:(h + 1) * qp, :].astype(o_ref.dtype)


def _sb_sample(proj, bias, cache_k, cache_v, page_table, layer, heads):
    d = SB_DIM
    run = SB_RUN
    batch, n_pages = page_table.shape
    depth, n_phys, page = cache_k.shape[:3]
    w = heads * d
    qp = SAMPLE_PAD
    pps = _tile(n_pages, 16)
    assert (pps * page) % run == 0
    steps = n_pages // pps
    ck = cache_k.reshape(depth * n_phys, page * heads, d)
    cv = cache_v.reshape(depth * n_phys, page * heads, d)

    def page_spec(i):
        def index(b, s, pt):
            return (layer * n_phys + pt[b * n_pages + (n_pages - 1 - (s * pps + i))], 0, 0)
        return pl.BlockSpec((None, page * heads, d), index)

    page_specs = []
    page_args = []
    for i in range(pps):
        page_specs += [page_spec(i), page_spec(i)]
        page_args += [ck, cv]
    grid_spec = pltpu.PrefetchScalarGridSpec(
        num_scalar_prefetch=1,
        grid=(batch, steps),
        in_specs=[
            pl.BlockSpec(memory_space=pltpu.SMEM),
            pl.BlockSpec((qp, w), lambda b, s, pt: (b, 0)),
            pl.BlockSpec((qp, w), lambda b, s, pt: (b, 1)),
            pl.BlockSpec((qp, w), lambda b, s, pt: (b, 2)),
            pl.BlockSpec((2 * run, run), lambda b, s, pt: (0, 0)),
        ] + page_specs,
        out_specs=pl.BlockSpec((qp, w), lambda b, s, pt: (b, 0)),
        scratch_shapes=[pltpu.VMEM((heads * qp, d), F32), pltpu.VMEM((heads * qp, LANES), F32)],
    )
    return pl.pallas_call(
        functools.partial(_sb_sample_body, heads=heads, pages_per_step=pps, page=page, scale=d ** -0.5),
        out_shape=jax.ShapeDtypeStruct((batch * qp, w), BF16),
        grid_spec=grid_spec,
        compiler_params=_params("parallel", "arbitrary"),
        name="sb_sample",
    )(page_table.reshape(-1).astype(jnp.int32), bias.astype(F32), proj, proj, proj,
      _suffix_sum_operand(run), *page_args)


_NN = (((1,), (0,)), ((), ()))
_NT = (((1,), (1,)), ((), ()))
_TN = (((0,), (0,)), ((), ()))


def _split_bf16(x):
    hi = x.astype(BF16)
    return hi, (x - hi.astype(F32)).astype(BF16)


def _mm(a, b, dims, precise):
    dot = lambda x, y: lax.dot_general(x, y, dims, preferred_element_type=F32)
    if not precise:
        return dot(a.astype(BF16), b.astype(BF16))
    ah, al = _split_bf16(a)
    bh, bl = _split_bf16(b)
    return dot(ah, bh) + dot(ah, bl) + dot(al, bh)


def _split3_bf16(x):
    hi = x.astype(BF16)
    rest = x - hi.astype(F32)
    mid = rest.astype(BF16)
    return hi, mid, (rest - mid.astype(F32)).astype(BF16)


def _mm_exact_lhs(a, b):
    a = a.astype(BF16)
    return sum(jnp.dot(a, piece, preferred_element_type=F32) for piece in _split3_bf16(b))


def _mm_exact_rhs(a, b):
    b = b.astype(BF16)
    return sum(jnp.dot(piece, b, preferred_element_type=F32) for piece in _split3_bf16(a))


def _unit_lower_inverses(ms):
    c = ms[0].shape[0]
    eye = (lax.broadcasted_iota(jnp.int32, (c, c), 0) == lax.broadcasted_iota(jnp.int32, (c, c), 1)).astype(F32)
    ps = [-m for m in ms]
    invs = [eye + p for p in ps]
    span = 2
    while span < c:
        ps = [_mm(p, p, _NN, True) for p in ps]
        invs = [inv + _mm(inv, p, _NN, True) for inv, p in zip(invs, ps)]
        span *= 2
    return invs


def _gdn_body(qkv_ref, z_ref, ba_ref, bat_ref, prev_ref, s0_ref, wconv_ref, arow_ref, dtrow_ref,
              acol_ref, dtcol_ref, gnorm_ref, o_ref, s_ref, tail_ref, *, heads, chunk, n_valid, precise):
    t = pl.program_id(1)
    tt = qkv_ref.shape[0]
    kw = heads * GDN_DK
    sub = SUBLANES

    @pl.when(t == 0)
    def _():
        tail_ref[...] = prev_ref[...]
        s_ref[...] = s0_ref[...]

    x = qkv_ref[...]
    head_rows = jnp.concatenate([tail_ref[...], x[:sub]], axis=0)
    y = x * wconv_ref[CONV_W - 1:CONV_W, :]
    for shift in range(1, CONV_W):
        first = pltpu.roll(head_rows, shift, 0)[sub:]
        if tt > sub:
            shifted = jnp.concatenate([first, pltpu.roll(x, shift, 0)[sub:]], axis=0)
        else:
            shifted = first
        y = y + shifted * wconv_ref[CONV_W - 1 - shift:CONV_W - shift, :]
    tail_ref[...] = x[tt - sub:]
    y = y * _sigmoid(y)

    n_chunks = max(tt, chunk) // chunk

    def rows(a, c):
        if tt < chunk:
            return jnp.concatenate([a, jnp.zeros((chunk - tt, a.shape[1]), a.dtype)], axis=0)
        return a[c * chunk:(c + 1) * chunk]

    ri = lax.broadcasted_iota(jnp.int32, (chunk, chunk), 0)
    ci = lax.broadcasted_iota(jnp.int32, (chunk, chunk), 1)
    incl = ri >= ci
    strict = ri > ci
    tri = incl.astype(F32)
    tri_t = (ri <= ci).astype(F32)

    items = []
    for c in range(n_chunks):
        ba = rows(ba_ref[...], c)
        g_tok = -jnp.exp(arow_ref[...]) * _softplus(ba + dtrow_ref[...])
        beta_tok = _sigmoid(ba)
        bat = bat_ref[:, c * chunk:(c + 1) * chunk]
        g_row = -jnp.exp(acol_ref[...]) * _softplus(bat + dtcol_ref[...])
        if n_valid is not None:
            valid_r = lax.broadcasted_iota(jnp.int32, g_tok.shape, 0) < n_valid
            g_tok = jnp.where(valid_r, g_tok, 0.0)
            beta_tok = jnp.where(valid_r, beta_tok, 0.0)
            g_row = jnp.where(lax.broadcasted_iota(jnp.int32, g_row.shape, 1) < n_valid, g_row, 0.0)
        gc_tok = _mm_exact_lhs(tri, g_tok)
        gc_row = _mm_exact_rhs(g_row, tri_t)
        eg_tok = jnp.exp(gc_tok)
        yc = rows(y, c)
        for h in range(heads):
            gcol = gc_tok[:, heads + h:heads + h + 1]
            grow = gc_row[heads + h:heads + h + 1, :]
            beta = beta_tok[:, h:h + 1]
            eg = eg_tok[:, heads + h:heads + h + 1]
            q = yc[:, h * GDN_DK:(h + 1) * GDN_DK]
            k = yc[:, kw + h * GDN_DK:kw + (h + 1) * GDN_DK]
            v = yc[:, 2 * kw + h * GDN_DV:2 * kw + (h + 1) * GDN_DV]
            q = q * lax.rsqrt(jnp.sum(q * q, axis=-1, keepdims=True) + EPS) * (GDN_DK ** -0.5)
            k = k * lax.rsqrt(jnp.sum(k * k, axis=-1, keepdims=True) + EPS)
            g_last = gcol[chunk - 1:chunk, :]
            items.append(dict(
                c=c, h=h, q=q, k=k, kb=k * beta, g_last=g_last,
                decay=jnp.where(incl, jnp.exp(jnp.where(incl, gcol - grow, 0.0)), 0.0),
                rhs=jnp.concatenate([v * beta, (k * beta) * eg], axis=1),
                q_dec=q * eg, k_dec=k * jnp.exp(g_last - gcol)))
    kks = [_mm(it["kb"], it["k"], _NT, precise) for it in items]
    qks = [_mm(it["q"], it["k"], _NT, precise) for it in items]
    invs = _unit_lower_inverses([jnp.where(strict, kk * it["decay"], 0.0) for kk, it in zip(kks, items)])
    uws = [_mm(inv, it["rhs"], _NN, True) for inv, it in zip(invs, items)]
    qks = [jnp.where(incl, qk * it["decay"], 0.0) for qk, it in zip(qks, items)]

    for c in range(n_chunks):
        sel = [i for i, it in enumerate(items) if it["c"] == c]
        states = [s_ref[items[i]["h"]] for i in sel]
        ws = [_mm(jnp.concatenate([uws[i][:, GDN_DV:], items[i]["q_dec"]], axis=0), s, _NN, precise)
              for i, s in zip(sel, states)]
        v_news = [uws[i][:, :GDN_DV] - w[:chunk] for i, w in zip(sel, ws)]
        outs = [w[chunk:] + _mm(qks[i], vn, _NN, precise) for i, w, vn in zip(sel, ws, v_news)]
        deltas = [_mm(items[i]["k_dec"], vn, _TN, precise) for i, vn in zip(sel, v_news)]
        zc = rows(z_ref[...], c)
        for i, s, o, delta in zip(sel, states, outs, deltas):
            h = items[i]["h"]
            s_ref[h] = s * jnp.exp(items[i]["g_last"]) + delta
            o = o * lax.rsqrt(jnp.mean(o * o, axis=-1, keepdims=True) + EPS) * gnorm_ref[...]
            zz = zc[:, h * GDN_DV:(h + 1) * GDN_DV]
            out = (o * (zz * _sigmoid(zz))).astype(o_ref.dtype)
            if tt < chunk:
                o_ref[:, h * GDN_DV:(h + 1) * GDN_DV] = out[:tt]
            else:
                o_ref[c * chunk:(c + 1) * chunk, h * GDN_DV:(h + 1) * GDN_DV] = out


def _gdn(proj, ba, prev, s0, w_conv, a_log, dt_bias, gdn_norm, batch, seq, heads, qkv_block, z_block,
         n_valid, precise):
    kw = heads * GDN_DK
    vw = heads * GDN_DV
    cw = 2 * kw + vw
    chunk = GDN_CHUNK
    tt = _tile(seq, 128)
    assert tt % chunk == 0 or tt < chunk
    nt = seq // tt
    bat = jnp.swapaxes(ba[:, :2 * SUBLANES].reshape(batch, seq, 2 * SUBLANES), 1, 2)
    if seq < chunk:
        bat = jnp.pad(bat, ((0, 0), (0, 0), (0, chunk - seq)))
    arow = jnp.zeros((1, LANES), F32).at[0, heads:2 * heads].set(a_log.astype(F32))
    dtrow = jnp.zeros((1, LANES), F32).at[0, heads:2 * heads].set(dt_bias.astype(F32))
    acol = arow[0, :2 * SUBLANES].reshape(2 * SUBLANES, 1)
    dtcol = dtrow[0, :2 * SUBLANES].reshape(2 * SUBLANES, 1)
    const = lambda b, t: (0, 0)
    o, s_new = pl.pallas_call(
        functools.partial(_gdn_body, heads=heads, chunk=chunk, n_valid=n_valid, precise=precise),
        out_shape=(jax.ShapeDtypeStruct((batch * seq, vw), BF16),
                   jax.ShapeDtypeStruct((batch, heads, GDN_DK, GDN_DV), F32)),
        grid=(batch, nt),
        in_specs=[
            pl.BlockSpec((tt, cw), lambda b, t: (b * nt + t, qkv_block)),
            pl.BlockSpec((tt, vw), lambda b, t: (b * nt + t, z_block)),
            pl.BlockSpec((tt, LANES), lambda b, t: (b * nt + t, 0)),
            pl.BlockSpec((None, 2 * SUBLANES, max(tt, chunk)), lambda b, t: (b, 0, t)),
            pl.BlockSpec((None, SUBLANES, cw), lambda b, t: (b, 0, 0)),
            pl.BlockSpec((None, heads, GDN_DK, GDN_DV), lambda b, t: (b, 0, 0, 0)),
            pl.BlockSpec((CONV_W, cw), const),
            pl.BlockSpec((1, LANES), const),
            pl.BlockSpec((1, LANES), const),
            pl.BlockSpec((2 * SUBLANES, 1), const),
            pl.BlockSpec((2 * SUBLANES, 1), const),
            pl.BlockSpec((1, GDN_DV), const),
        ],
        out_specs=(pl.BlockSpec((tt, vw), lambda b, t: (b * nt + t, 0)),
                   pl.BlockSpec((None, heads, GDN_DK, GDN_DV), lambda b, t: (b, 0, 0, 0))),
        scratch_shapes=[pltpu.VMEM((SUBLANES, cw), F32)],
        compiler_params=_params("parallel", "arbitrary"),
        name="gdn",
    )(proj, proj, ba, bat, prev, s0, w_conv.astype(F32), arow, dtrow, acol, dtcol,
      gdn_norm.reshape(1, GDN_DV).astype(F32))
    return o, s_new


def _mem_attn_body(q_ref, k_ref, v_ref, g_ref, o_ref, *, heads, dh):
    scale = dh ** -0.5
    for h in range(heads):
        hs = slice(h * dh, (h + 1) * dh)
        q = q_ref[:, hs]
        q = q * lax.rsqrt(jnp.mean(q * q, axis=-1, keepdims=True) + EPS) * g_ref[...]
        s = _dot_nt(q, k_ref[:, hs]) * scale
        e = jnp.exp(s - jnp.max(s, axis=-1, keepdims=True))
        p = e / jnp.sum(e, axis=-1, keepdims=True)
        o_ref[:, hs] = _dot(p, v_ref[:, hs]).astype(o_ref.dtype)


def _mem_attn(proj, mem_k, mem_v, q_norm, batch, seq, q_block):
    n_mem, heads, dh = mem_k.shape[1:]
    w = heads * dh
    tt = _tile(seq, 512)
    nt = seq // tt
    return pl.pallas_call(
        functools.partial(_mem_attn_body, heads=heads, dh=dh),
        out_shape=jax.ShapeDtypeStruct((batch * seq, w), BF16),
        grid=(batch, nt),
        in_specs=[
            pl.BlockSpec((tt, w), lambda b, t: (b * nt + t, q_block)),
            pl.BlockSpec((None, n_mem, w), lambda b, t: (b, 0, 0)),
            pl.BlockSpec((None, n_mem, w), lambda b, t: (b, 0, 0)),
            pl.BlockSpec((1, dh), lambda b, t: (0, 0)),
        ],
        out_specs=pl.BlockSpec((tt, w), lambda b, t: (b * nt + t, 0)),
        compiler_params=_params("parallel", "parallel"),
        name="mem_attn",
    )(proj, mem_k.reshape(batch, n_mem, w), mem_v.reshape(batch, n_mem, w), q_norm.reshape(1, dh).astype(F32))


def _merge_body(a0_ref, a1_ref, a2_ref, w0_ref, w1_ref, w2_ref, g0_ref, g1_ref, g2_ref, o_ref):
    acc = _sigmoid(g0_ref[...]) * jnp.dot(a0_ref[...], w0_ref[...], preferred_element_type=F32)
    acc += _sigmoid(g1_ref[...]) * jnp.dot(a1_ref[...], w1_ref[...], preferred_element_type=F32)
    acc += _sigmoid(g2_ref[...]) * jnp.dot(a2_ref[...], w2_ref[...], preferred_element_type=F32)
    o_ref[...] = acc.astype(o_ref.dtype)


def _merge(branches, weights, proj, gate_col, d_model):
    m = proj.shape[0]
    tm = _tile(m, 1024)
    tn = _tile(d_model, 512)
    a_specs = [pl.BlockSpec((tm, a.shape[1]), lambda i, j: (i, 0)) for a in branches]
    w_specs = [pl.BlockSpec((w.shape[0], tn), lambda i, j: (0, j)) for w in weights]
    g_specs = [pl.BlockSpec((tm, tn), functools.partial(lambda i, j, off: (i, off + j), off=(gate_col + g * d_model) // tn))
               for g in range(3)]
    return pl.pallas_call(
        _merge_body,
        out_shape=jax.ShapeDtypeStruct((m, d_model), BF16),
        grid=(m // tm, d_model // tn),
        in_specs=a_specs + w_specs + g_specs,
        out_specs=pl.BlockSpec((tm, tn), lambda i, j: (i, j)),
        compiler_params=_params("parallel", "arbitrary"),
        name="merge",
    )(*branches, *weights, proj, proj, proj)


def _ffn_body(x_ref, g_ref, wg_ref, wu_ref, wo_ref, o_ref, h_ref, *, splits):
    f = pl.program_id(1)

    @pl.when(f == 0)
    def _():
        x = x_ref[...]
        ms = jnp.mean(x * x, axis=-1, keepdims=True)
        h_ref[...] = (x * lax.rsqrt(ms + EPS) * g_ref[...]).astype(BF16)
        o_ref[...] = x

    h = h_ref[...]
    gate = jnp.dot(h, wg_ref[...], preferred_element_type=F32)
    up = jnp.dot(h, wu_ref[...], preferred_element_type=F32)
    act = (gate * _sigmoid(gate) * up).astype(BF16)
    w = o_ref.shape[1] // splits
    for s in range(splits):
        o_ref[:, s * w:(s + 1) * w] += jnp.dot(act, wo_ref[:, s * w:(s + 1) * w], preferred_element_type=F32)


def _ffn(x, g, w_in, w_out, splits=4):
    m, d = x.shape
    dff = w_out.shape[0]
    tm = _tile(m, 512)
    tf = _tile(dff, 512)
    nf = dff // tf
    return pl.pallas_call(
        functools.partial(_ffn_body, splits=splits),
        out_shape=jax.ShapeDtypeStruct((m, d), F32),
        grid=(m // tm, nf),
        in_specs=[
            pl.BlockSpec((tm, d), lambda i, f: (i, 0)),
            pl.BlockSpec((1, d), lambda i, f: (0, 0)),
            pl.BlockSpec((d, tf), lambda i, f: (0, f)),
            pl.BlockSpec((d, tf), lambda i, f: (0, nf + f)),
            pl.BlockSpec((tf, d), lambda i, f: (f, 0)),
        ],
        out_specs=pl.BlockSpec((tm, d), lambda i, f: (i, 0)),
        scratch_shapes=[pltpu.VMEM((tm, d), BF16)],
        compiler_params=_params("parallel", "arbitrary"),
        name="ffn",
    )(x, g.reshape(1, d).astype(F32), w_in, w_in, w_out)


def _trunk(x, p, sb_attend, conv_prev, s0, mem_k, mem_v, n_valid, precise_gdn):
    batch, seq, d = x.shape
    heads = p["sb_heads"]
    x2 = x.reshape(batch * seq, d)
    h = _rmsnorm(x2, p["ln_mix"])
    proj, ba = _matmul_with_side(h, p["w_main"], p["w_ba"])
    o_sb = sb_attend(proj)
    o_gdn, s_new = _gdn(proj, ba, conv_prev, s0, p["w_conv"], p["a_log"], p["dt_bias"], p["gdn_norm"],
                        batch, seq, p["gdn_heads"], qkv_block=1, z_block=p["z_block"],
                        n_valid=n_valid, precise=precise_gdn)
    o_mem = _mem_attn(proj, mem_k, mem_v, p["q_norm_mem"], batch, seq, q_block=p["memq_block"])
    merged = _merge([o_sb, o_gdn, o_mem], [p["w_proj_sb"], p["w_proj_gdn"], p["w_proj_mem"]],
                    proj, p["gate_col"], d)
    x1 = _matmul(merged, p["w_out"], res=x2)
    y = _ffn(x1, p["ln_ffn"], p["w_ffn_in"], p["w_ffn_out"])
    return y.reshape(batch, seq, d), proj, s_new


def kernel(x_prompt, x_sample, mem_prompt, cache_sb_k, cache_sb_v, state_gdn, state_conv, cache_mem_k, cache_mem_v, page_table, ln_mix, w_in, sb_bias, w_conv, a_log, dt_bias, gdn_norm, q_norm_mem, k_norm_mem, ln_mem, w_mem_kv, w_proj_sb, w_proj_gdn, w_proj_mem, w_out, ln_ffn, w_ffn_in, w_ffn_out):
    depth = w_in.shape[0]
    d_model = x_prompt.shape[-1]
    bp, tp = x_prompt.shape[:2]
    bs, ts = x_sample.shape[:2]
    sb_heads = sb_bias.shape[-1]
    gdn_heads = a_log.shape[-1]
    mem_heads, mem_dim = cache_mem_k.shape[-2:]
    n_mem = mem_prompt.shape[1]
    sb_w = sb_heads * SB_DIM
    kw = gdn_heads * GDN_DK
    vw = gdn_heads * GDN_DV
    cw = 2 * kw + vw
    mem_w = mem_heads * mem_dim
    assert sb_w == kw == vw == mem_w and cw == 3 * sb_w and 2 * gdn_heads <= 2 * SUBLANES
    off_ba = 3 * sb_w + cw + vw
    off_memq = off_ba + 2 * gdn_heads
    main_w = off_ba + mem_w + 3 * d_model
    assert w_in.shape[-1] == off_memq + mem_w + 3 * d_model and main_w % LANES == 0

    yp = x_prompt
    ys = jnp.pad(x_sample, ((0, 0), (0, SAMPLE_PAD - ts), (0, 0)))
    outs = [[] for _ in range(10)]
    for l in range(depth):
        w_main = jnp.concatenate([w_in[l][:, :off_ba], w_in[l][:, off_memq:]], axis=1).astype(BF16)
        w_ba = jnp.pad(w_in[l][:, off_ba:off_memq], ((0, 0), (0, LANES - 2 * gdn_heads))).astype(BF16)
        p = {
            "sb_heads": sb_heads, "gdn_heads": gdn_heads,
            "z_block": (3 * sb_w + cw) // vw, "memq_block": off_ba // mem_w, "gate_col": off_ba + mem_w,
            "ln_mix": ln_mix[l], "w_main": w_main, "w_ba": w_ba, "w_conv": w_conv[l], "a_log": a_log[l],
            "dt_bias": dt_bias[l], "gdn_norm": gdn_norm[l], "q_norm_mem": q_norm_mem[l],
            "w_proj_sb": w_proj_sb[l].astype(BF16), "w_proj_gdn": w_proj_gdn[l].astype(BF16),
            "w_proj_mem": w_proj_mem[l].astype(BF16), "w_out": w_out[l].astype(BF16), "ln_ffn": ln_ffn[l],
            "w_ffn_in": w_ffn_in[l].astype(BF16), "w_ffn_out": w_ffn_out[l].astype(BF16),
        }

        kv = _matmul(_rmsnorm(mem_prompt.reshape(bp * n_mem, d_model), ln_mem[l]), w_mem_kv[l].astype(BF16))
        mk = _headnorm(kv, k_norm_mem[l], mem_heads, 0).reshape(bp, n_mem, mem_heads, mem_dim)
        mv = kv[:, mem_w:].reshape(bp, n_mem, mem_heads, mem_dim)
        yp, proj_p, sp = _trunk(
            yp, p, lambda proj: _sb_prompt(proj, sb_bias[l], bp, tp, sb_heads),
            jnp.zeros((bp, SUBLANES, cw), F32), jnp.zeros((bp, gdn_heads, GDN_DK, GDN_DV), F32),
            mk, mv, n_valid=None, precise_gdn=False)

        conv_prev = jnp.pad(state_conv[l], ((0, 0), (SUBLANES - (CONV_W - 1), 0), (0, 0)))
        ys, proj_s, ss = _trunk(
            ys, p, lambda proj: _sb_sample(proj, sb_bias[l], cache_sb_k, cache_sb_v, page_table, l, sb_heads),
            conv_prev, state_gdn[l], cache_mem_k[l], cache_mem_v[l], n_valid=ts, precise_gdn=True)

        pp = proj_p.reshape(bp, tp, main_w)
        ps = proj_s.reshape(bs, SAMPLE_PAD, main_w)
        outs[0].append(pp[:, :, sb_w:2 * sb_w].reshape(bp, tp, sb_heads, SB_DIM))
        outs[1].append(pp[:, :, 2 * sb_w:3 * sb_w].reshape(bp, tp, sb_heads, SB_DIM))
        outs[2].append(ps[:, :ts, sb_w:2 * sb_w].reshape(bs, ts, sb_heads, SB_DIM))
        outs[3].append(ps[:, :ts, 2 * sb_w:3 * sb_w].reshape(bs, ts, sb_heads, SB_DIM))
        outs[4].append(sp)
        outs[5].append(ss)
        outs[6].append(pp[:, tp - (CONV_W - 1):, 3 * sb_w:3 * sb_w + cw])
        outs[7].append(ps[:, ts - (CONV_W - 1):ts, 3 * sb_w:3 * sb_w + cw])
        outs[8].append(mk)
        outs[9].append(mv)
    return (yp, ys[:, :ts]) + tuple(jnp.stack(o) for o in outs)
```

```python
import functools

import jax
import jax.numpy as jnp
from jax import lax
from jax.experimental import pallas as pl
from jax.experimental.pallas import tpu as pltpu

F32 = jnp.float32
BF16 = jnp.bfloat16

EPS = 1e-6
LOG2E = 1.4426950408889634
SB_DIM = 128
SB_RUN = 256
GDN_DK = 128
GDN_DV = 128
GDN_CHUNK = 64
CONV_W = 4
SUBLANES = 8
LANES = 128
SAMPLE_PAD = 2 * SUBLANES
VMEM_LIMIT = 56 * 1024 * 1024


def _params(*sem):
    return pltpu.CompilerParams(dimension_semantics=sem, vmem_limit_bytes=VMEM_LIMIT)


def _tile(n, pref):
    if n <= pref:
        return n
    t = pref
    while n % t:
        t //= 2
    return t


def _dot(a, b):
    return jnp.dot(a.astype(BF16), b.astype(BF16), preferred_element_type=F32)


def _dot_nt(a, b):
    return lax.dot_general(a.astype(BF16), b.astype(BF16), (((1,), (1,)), ((), ())), preferred_element_type=F32)


def _sigmoid(x):
    return 1.0 / (1.0 + jnp.exp(-x))


def _softplus(x):
    return jnp.maximum(x, 0.0) + jnp.log(1.0 + jnp.exp(-jnp.abs(x)))


def _rmsnorm_body(x_ref, g_ref, o_ref):
    x = x_ref[...].astype(F32)
    ms = jnp.mean(x * x, axis=-1, keepdims=True)
    o_ref[...] = (x * lax.rsqrt(ms + EPS) * g_ref[...]).astype(o_ref.dtype)


def _rmsnorm(x, g, out_dtype=BF16):
    m, d = x.shape
    tm = _tile(m, 512)
    return pl.pallas_call(
        _rmsnorm_body,
        out_shape=jax.ShapeDtypeStruct((m, d), out_dtype),
        grid=(m // tm,),
        in_specs=[pl.BlockSpec((tm, d), lambda i: (i, 0)), pl.BlockSpec((1, d), lambda i: (0, 0))],
        out_specs=pl.BlockSpec((tm, d), lambda i: (i, 0)),
        compiler_params=_params("parallel"),
        name="rmsnorm",
    )(x, g.reshape(1, d).astype(F32))


def _headnorm_body(x_ref, g_ref, o_ref, *, heads, dh):
    for h in range(heads):
        x = x_ref[:, h * dh:(h + 1) * dh].astype(F32)
        ms = jnp.mean(x * x, axis=-1, keepdims=True)
        o_ref[:, h * dh:(h + 1) * dh] = (x * lax.rsqrt(ms + EPS) * g_ref[...]).astype(o_ref.dtype)


def _headnorm(x, g, heads, col_block):
    m = x.shape[0]
    dh = g.shape[-1]
    w = heads * dh
    tm = _tile(m, 512)
    return pl.pallas_call(
        functools.partial(_headnorm_body, heads=heads, dh=dh),
        out_shape=jax.ShapeDtypeStruct((m, w), F32),
        grid=(m // tm,),
        in_specs=[pl.BlockSpec((tm, w), lambda i: (i, col_block)), pl.BlockSpec((1, dh), lambda i: (0, 0))],
        out_specs=pl.BlockSpec((tm, w), lambda i: (i, 0)),
        compiler_params=_params("parallel"),
        name="headnorm",
    )(x, g.reshape(1, dh).astype(F32))


def _mm_body(a_ref, b_ref, o_ref):
    o_ref[...] = jnp.dot(a_ref[...], b_ref[...], preferred_element_type=F32).astype(o_ref.dtype)


def _mm_res_body(a_ref, b_ref, r_ref, o_ref):
    o_ref[...] = (r_ref[...] + jnp.dot(a_ref[...], b_ref[...], preferred_element_type=F32)).astype(o_ref.dtype)


def _mm_side_body(a_ref, b_ref, s_ref, o_ref, so_ref):
    o_ref[...] = jnp.dot(a_ref[...], b_ref[...], preferred_element_type=F32)

    @pl.when(pl.program_id(1) == 0)
    def _():
        so_ref[...] = jnp.dot(a_ref[...], s_ref[...], preferred_element_type=F32)


def _matmul_with_side(a, b, side):
    m, k = a.shape
    n = b.shape[1]
    ns = side.shape[1]
    tm = _tile(m, 1024)
    tn = _tile(n, 512)
    return pl.pallas_call(
        _mm_side_body,
        out_shape=(jax.ShapeDtypeStruct((m, n), F32), jax.ShapeDtypeStruct((m, ns), F32)),
        grid=(m // tm, n // tn),
        in_specs=[pl.BlockSpec((tm, k), lambda i, j: (i, 0)), pl.BlockSpec((k, tn), lambda i, j: (0, j)),
                  pl.BlockSpec((k, ns), lambda i, j: (0, 0))],
        out_specs=(pl.BlockSpec((tm, tn), lambda i, j: (i, j)), pl.BlockSpec((tm, ns), lambda i, j: (i, 0))),
        compiler_params=_params("parallel", "arbitrary"),
        name="matmul_side",
    )(a, b, side)


def _matmul(a, b, res=None, out_dtype=F32):
    m, k = a.shape
    n = b.shape[1]
    tm = _tile(m, 1024)
    tn = _tile(n, 512)
    in_specs = [pl.BlockSpec((tm, k), lambda i, j: (i, 0)), pl.BlockSpec((k, tn), lambda i, j: (0, j))]
    args = [a, b]
    body = _mm_body
    if res is not None:
        in_specs.append(pl.BlockSpec((tm, tn), lambda i, j: (i, j)))
        args.append(res)
        body = _mm_res_body
    return pl.pallas_call(
        body,
        out_shape=jax.ShapeDtypeStruct((m, n), out_dtype),
        grid=(m // tm, n // tn),
        in_specs=in_specs,
        out_specs=pl.BlockSpec((tm, tn), lambda i, j: (i, j)),
        compiler_params=_params("parallel", "arbitrary"),
        name="matmul",
    )(*args)


def _sb_runs(items, carries, suffix):
    n = len(items)
    scores, sums, out = {}, {}, {}

    def stack(parts):
        return parts[0] if len(parts) == 1 else jnp.concatenate(parts, axis=0)

    def emit_scores(i):
        _, qs, ks, _, biases, _ = items[i]
        scores[i] = stack([_dot_nt(q, k) + bias for q, k, bias in zip(qs, ks, biases)])

    def emit_sums(i):
        mask = items[i][5]
        z = scores.pop(i)
        neg_abs = lax.bitcast_convert_type(lax.bitcast_convert_type(z, jnp.uint32) | jnp.uint32(0x80000000), F32)
        sp = jnp.maximum(z, 0.0) + jnp.log2(1.0 + jnp.exp2(neg_abs))
        if mask is not None:
            sp = jnp.where(mask, sp, 0.0)
        hi = sp.astype(BF16)
        lo = (sp - hi.astype(F32)).astype(BF16)
        incl = jnp.dot(jnp.concatenate([hi, lo], axis=1), suffix, preferred_element_type=F32)
        sums[i] = (z, incl)

    def emit_values(i):
        group, qs, _, vs, _, mask = items[i]
        z, incl = sums.pop(i)
        carry = carries[group]
        a = jnp.exp2(z - incl - jnp.concatenate([carry] * (z.shape[1] // LANES), axis=1))
        if mask is not None:
            a = jnp.where(mask, a, 0.0)
        carries[group] = carry + jnp.broadcast_to(incl[:, 0:1], carry.shape)
        a = a.astype(BF16)
        r = qs[0].shape[0]
        contrib = stack([jnp.dot(a[j * r:(j + 1) * r], v, preferred_element_type=F32) for j, v in enumerate(vs)])
        out[group] = contrib if group not in out else out[group] + contrib

    for step in range(n + 3):
        if step < n:
            emit_scores(step)
        if 0 <= step - 1 < n:
            emit_sums(step - 1)
        if 0 <= step - 3 < n:
            emit_values(step - 3)
    return out


def _suffix_sum_operand(run):
    j = lax.broadcasted_iota(jnp.int32, (2 * run, run), 0) % run
    s = lax.broadcasted_iota(jnp.int32, (2 * run, run), 1)
    return jnp.where(j >= s, 1.0, 0.0).astype(BF16)


def _sb_prompt_body(bias_ref, q_ref, k_ref, v_ref, suffix_ref, o_ref, kb_ref, vb_ref, qs_ref, acc_ref, carry_ref,
                    *, tq, rb, kb, scale):
    h = pl.program_id(1)
    qi = pl.program_id(2)
    run = suffix_ref.shape[1]

    @pl.when(qi == 0)
    def _():
        kb_ref[...] = k_ref[...].astype(BF16)
        vb_ref[...] = v_ref[...].astype(BF16)

    qs_ref[...] = (q_ref[...] * (scale * LOG2E)).astype(BF16)
    acc_ref[...] = jnp.zeros_like(acc_ref)
    carry_ref[...] = jnp.zeros_like(carry_ref)
    bias = bias_ref[h] * LOG2E
    suffix = suffix_ref[...]
    groups = list(range(tq // rb))

    def update(first_run, n, diagonal):
        items = []
        for r in reversed(range(n)):
            start = pl.multiple_of((first_run + r) * run, run)
            k = kb_ref[pl.ds(start, run), :]
            v = vb_ref[pl.ds(start, run), :]
            for g in groups:
                if diagonal and r > g:
                    continue
                items.append((g, [qs_ref[g * rb:(g + 1) * rb]], [k], [v], [bias],
                              causal if diagonal and r == g else None))
        carries = {g: carry_ref[g * rb:(g + 1) * rb] for g in groups}
        for g, contrib in _sb_runs(items, carries, suffix).items():
            acc_ref[g * rb:(g + 1) * rb] += contrib
            carry_ref[g * rb:(g + 1) * rb] = carries[g]

    causal = lax.broadcasted_iota(jnp.int32, (rb, run), 1) < lax.broadcasted_iota(jnp.int32, (rb, run), 0)
    update(qi * (tq // run), tq // run, True)

    n_trips = (qi * tq) // (kb * run)

    def body(jj, c):
        update((n_trips - 1 - jj) * kb, kb, False)
        return c

    lax.fori_loop(0, n_trips, body, 0)
    o_ref[...] = acc_ref[...].astype(o_ref.dtype)


def _sb_prompt(proj, bias, batch, seq, heads, tq=1024, kb=4):
    d = SB_DIM
    run = SB_RUN
    rb = run
    tq = _tile(seq, tq)
    kb = min(kb, tq // run)
    assert tq % (kb * run) == 0 and tq % rb == 0
    nq = seq // tq
    return pl.pallas_call(
        functools.partial(_sb_prompt_body, tq=tq, rb=rb, kb=kb, scale=d ** -0.5),
        out_shape=jax.ShapeDtypeStruct((batch * seq, heads * d), BF16),
        grid=(batch, heads, nq),
        in_specs=[
            pl.BlockSpec(memory_space=pltpu.SMEM),
            pl.BlockSpec((tq, d), lambda b, h, i: (b * nq + i, h)),
            pl.BlockSpec((seq, d), lambda b, h, i: (b, heads + h)),
            pl.BlockSpec((seq, d), lambda b, h, i: (b, 2 * heads + h)),
            pl.BlockSpec((2 * run, run), lambda b, h, i: (0, 0)),
        ],
        out_specs=pl.BlockSpec((tq, d), lambda b, h, i: (b * nq + i, h)),
        scratch_shapes=[pltpu.VMEM((seq, d), BF16), pltpu.VMEM((seq, d), BF16), pltpu.VMEM((tq, d), BF16),
                        pltpu.VMEM((tq, d), F32), pltpu.VMEM((tq, LANES), F32)],
        compiler_params=_params("parallel", "parallel", "arbitrary"),
        name="sb_prompt",
    )(bias.astype(F32), proj, proj, proj, _suffix_sum_operand(run))


def _sb_sample_body(pt_ref, bias_ref, q_ref, kn_ref, vn_ref, suffix_ref, *refs, heads, pages_per_step, page, scale):
    del pt_ref
    page_refs = refs[:2 * pages_per_step]
    o_ref, acc_ref, carry_ref = refs[2 * pages_per_step:]
    s = pl.program_id(1)
    d = SB_DIM
    run = suffix_ref.shape[1]
    qp = q_ref.shape[0]
    q = (q_ref[...] * (scale * LOG2E)).astype(BF16)
    suffix = suffix_ref[...]
    qs = [q[:, h * d:(h + 1) * d] for h in range(heads)]
    biases = [bias_ref[h] * LOG2E for h in range(heads)]

    @pl.when(s == 0)
    def _():
        pad = jnp.zeros((run - qp, heads * d), F32)
        k_new = jnp.concatenate([kn_ref[...], pad], axis=0).astype(BF16)
        v_new = jnp.concatenate([vn_ref[...], pad], axis=0).astype(BF16)
        row = lax.broadcasted_iota(jnp.int32, (heads * qp, run), 0) % qp
        mask = lax.broadcasted_iota(jnp.int32, (heads * qp, run), 1) < row
        item = (0, qs, [k_new[:, h * d:(h + 1) * d] for h in range(heads)],
                [v_new[:, h * d:(h + 1) * d] for h in range(heads)], biases, mask)
        carries = {0: jnp.zeros((heads * qp, LANES), F32)}
        acc_ref[...] = _sb_runs([item], carries, suffix)[0]
        carry_ref[...] = carries[0]

    def head_rows(ref, h):
        return ref[pl.ds(h, page, stride=heads), :].astype(BF16)

    ppr = run // page
    items = []
    for first in range(0, pages_per_step, ppr):
        order = list(reversed(range(first, first + ppr)))
        ks = [jnp.concatenate([head_rows(page_refs[2 * i], h) for i in order], axis=0) for h in range(heads)]
        vs = [jnp.concatenate([head_rows(page_refs[2 * i + 1], h) for i in order], axis=0) for h in range(heads)]
        items.append((0, qs, ks, vs, biases, None))
    carries = {0: carry_ref[...]}
    acc_ref[...] += _sb_runs(items, carries, suffix)[0]
    carry_ref[...] = carries[0]

    @pl.when(s == pl.num_programs(1) - 1)
    def _():
        for h in range(heads):
            o_ref[:, h * d:(h + 1) * d] = acc_ref[h * qp:(h + 1) * qp, :].astype(o_ref.dtype)


def _sb_sample(proj, bias, cache_k, cache_v, page_table, layer, heads):
    d = SB_DIM
    run = SB_RUN
    batch, n_pages = page_table.shape
    depth, n_phys, page = cache_k.shape[:3]
    w = heads * d
    qp = SAMPLE_PAD
    pps = _tile(n_pages, 16)
    assert (pps * page) % run == 0
    steps = n_pages // pps
    ck = cache_k.reshape(depth * n_phys, page * heads, d)
    cv = cache_v.reshape(depth * n_phys, page * heads, d)

    def page_spec(i):
        def index(b, s, pt):
            return (layer * n_phys + pt[b * n_pages + (n_pages - 1 - (s * pps + i))], 0, 0)
        return pl.BlockSpec((None, page * heads, d), index)

    page_specs = []
    page_args = []
    for i in range(pps):
        page_specs += [page_spec(i), page_spec(i)]
        page_args += [ck, cv]
    grid_spec = pltpu.PrefetchScalarGridSpec(
        num_scalar_prefetch=1,
        grid=(batch, steps),
        in_specs=[
            pl.BlockSpec(memory_space=pltpu.SMEM),
            pl.BlockSpec((qp, w), lambda b, s, pt: (b, 0)),
            pl.BlockSpec((qp, w), lambda b, s, pt: (b, 1)),
            pl.BlockSpec((qp, w), lambda b, s, pt: (b, 2)),
            pl.BlockSpec((2 * run, run), lambda b, s, pt: (0, 0)),
        ] + page_specs,
        out_specs=pl.BlockSpec((qp, w), lambda b, s, pt: (b, 0)),
        scratch_shapes=[pltpu.VMEM((heads * qp, d), F32), pltpu.VMEM((heads * qp, LANES), F32)],
    )
    return pl.pallas_call(
        functools.partial(_sb_sample_body, heads=heads, pages_per_step=pps, page=page, scale=d ** -0.5),
        out_shape=jax.ShapeDtypeStruct((batch * qp, w), BF16),
        grid_spec=grid_spec,
        compiler_params=_params("parallel", "arbitrary"),
        name="sb_sample",
    )(page_table.reshape(-1).astype(jnp.int32), bias.astype(F32), proj, proj, proj,
      _suffix_sum_operand(run), *page_args)


_NN = (((1,), (0,)), ((), ()))
_NT = (((1,), (1,)), ((), ()))
_TN = (((0,), (0,)), ((), ()))


def _split_bf16(x):
    hi = x.astype(BF16)
    return hi, (x - hi.astype(F32)).astype(BF16)


def _mm(a, b, dims, precise):
    dot = lambda x, y: lax.dot_general(x, y, dims, preferred_element_type=F32)
    if not precise:
        return dot(a.astype(BF16), b.astype(BF16))
    ah, al = _split_bf16(a)
    bh, bl = _split_bf16(b)
    return dot(ah, bh) + dot(ah, bl) + dot(al, bh)


def _split3_bf16(x):
    hi = x.astype(BF16)
    rest = x - hi.astype(F32)
    mid = rest.astype(BF16)
    return hi, mid, (rest - mid.astype(F32)).astype(BF16)


def _mm_exact_lhs(a, b):
    a = a.astype(BF16)
    return sum(jnp.dot(a, piece, preferred_element_type=F32) for piece in _split3_bf16(b))


def _mm_exact_rhs(a, b):
    b = b.astype(BF16)
    return sum(jnp.dot(piece, b, preferred_element_type=F32) for piece in _split3_bf16(a))


def _unit_lower_inverses(ms):
    c = ms[0].shape[0]
    eye = (lax.broadcasted_iota(jnp.int32, (c, c), 0) == lax.broadcasted_iota(jnp.int32, (c, c), 1)).astype(F32)
    ps = [-m for m in ms]
    invs = [eye + p for p in ps]
    span = 2
    while span < c:
        ps = [_mm(p, p, _NN, True) for p in ps]
        invs = [inv + _mm(inv, p, _NN, True) for inv, p in zip(invs, ps)]
        span *= 2
    return invs


def _gdn_body(qkv_ref, z_ref, ba_ref, bat_ref, prev_ref, s0_ref, wconv_ref, arow_ref, dtrow_ref,
              acol_ref, dtcol_ref, gnorm_ref, o_ref, s_ref, tail_ref, *, heads, chunk, n_valid, precise):
    t = pl.program_id(1)
    tt = qkv_ref.shape[0]
    kw = heads * GDN_DK
    sub = SUBLANES

    @pl.when(t == 0)
    def _():
        tail_ref[...] = prev_ref[...]
        s_ref[...] = s0_ref[...]

    x = qkv_ref[...]
    head_rows = jnp.concatenate([tail_ref[...], x[:sub]], axis=0)
    y = x * wconv_ref[CONV_W - 1:CONV_W, :]
    for shift in range(1, CONV_W):
        first = pltpu.roll(head_rows, shift, 0)[sub:]
        if tt > sub:
            shifted = jnp.concatenate([first, pltpu.roll(x, shift, 0)[sub:]], axis=0)
        else:
            shifted = first
        y = y + shifted * wconv_ref[CONV_W - 1 - shift:CONV_W - shift, :]
    tail_ref[...] = x[tt - sub:]
    y = y * _sigmoid(y)

    n_chunks = max(tt, chunk) // chunk

    def rows(a, c):
        if tt < chunk:
            return jnp.concatenate([a, jnp.zeros((chunk - tt, a.shape[1]), a.dtype)], axis=0)
        return a[c * chunk:(c + 1) * chunk]

    ri = lax.broadcasted_iota(jnp.int32, (chunk, chunk), 0)
    ci = lax.broadcasted_iota(jnp.int32, (chunk, chunk), 1)
    incl = ri >= ci
    strict = ri > ci
    tri = incl.astype(F32)
    tri_t = (ri <= ci).astype(F32)

    items = []
    for c in range(n_chunks):
        ba = rows(ba_ref[...], c)
        g_tok = -jnp.exp(arow_ref[...]) * _softplus(ba + dtrow_ref[...])
        beta_tok = _sigmoid(ba)
        bat = bat_ref[:, c * chunk:(c + 1) * chunk]
        g_row = -jnp.exp(acol_ref[...]) * _softplus(bat + dtcol_ref[...])
        if n_valid is not None:
            valid_r = lax.broadcasted_iota(jnp.int32, g_tok.shape, 0) < n_valid
            g_tok = jnp.where(valid_r, g_tok, 0.0)
            beta_tok = jnp.where(valid_r, beta_tok, 0.0)
            g_row = jnp.where(lax.broadcasted_iota(jnp.int32, g_row.shape, 1) < n_valid, g_row, 0.0)
        gc_tok = _mm_exact_lhs(tri, g_tok)
        gc_row = _mm_exact_rhs(g_row, tri_t)
        eg_tok = jnp.exp(gc_tok)
        yc = rows(y, c)
        for h in range(heads):
            gcol = gc_tok[:, heads + h:heads + h + 1]
            grow = gc_row[heads + h:heads + h + 1, :]
            beta = beta_tok[:, h:h + 1]
            eg = eg_tok[:, heads + h:heads + h + 1]
            q = yc[:, h * GDN_DK:(h + 1) * GDN_DK]
            k = yc[:, kw + h * GDN_DK:kw + (h + 1) * GDN_DK]
            v = yc[:, 2 * kw + h * GDN_DV:2 * kw + (h + 1) * GDN_DV]
            q = q * lax.rsqrt(jnp.sum(q * q, axis=-1, keepdims=True) + EPS) * (GDN_DK ** -0.5)
            k = k * lax.rsqrt(jnp.sum(k * k, axis=-1, keepdims=True) + EPS)
            g_last = gcol[chunk - 1:chunk, :]
            items.append(dict(
                c=c, h=h, q=q, k=k, kb=k * beta, g_last=g_last,
                decay=jnp.where(incl, jnp.exp(jnp.where(incl, gcol - grow, 0.0)), 0.0),
                rhs=jnp.concatenate([v * beta, (k * beta) * eg], axis=1),
                q_dec=q * eg, k_dec=k * jnp.exp(g_last - gcol)))
    kks = [_mm(it["kb"], it["k"], _NT, precise) for it in items]
    qks = [_mm(it["q"], it["k"], _NT, precise) for it in items]
    invs = _unit_lower_inverses([jnp.where(strict, kk * it["decay"], 0.0) for kk, it in zip(kks, items)])
    uws = [_mm(inv, it["rhs"], _NN, True) for inv, it in zip(invs, items)]
    qks = [jnp.where(incl, qk * it["decay"], 0.0) for qk, it in zip(qks, items)]

    for c in range(n_chunks):
        sel = [i for i, it in enumerate(items) if it["c"] == c]
        states = [s_ref[items[i]["h"]] for i in sel]
        ws = [_mm(jnp.concatenate([uws[i][:, GDN_DV:], items[i]["q_dec"]], axis=0), s, _NN, precise)
              for i, s in zip(sel, states)]
        v_news = [uws[i][:, :GDN_DV] - w[:chunk] for i, w in zip(sel, ws)]
        outs = [w[chunk:] + _mm(qks[i], vn, _NN, precise) for i, w, vn in zip(sel, ws, v_news)]
        deltas = [_mm(items[i]["k_dec"], vn, _TN, precise) for i, vn in zip(sel, v_news)]
        zc = rows(z_ref[...], c)
        for i, s, o, delta in zip(sel, states, outs, deltas):
            h = items[i]["h"]
            s_ref[h] = s * jnp.exp(items[i]["g_last"]) + delta
            o = o * lax.rsqrt(jnp.mean(o * o, axis=-1, keepdims=True) + EPS) * gnorm_ref[...]
            zz = zc[:, h * GDN_DV:(h + 1) * GDN_DV]
            out = (o * (zz * _sigmoid(zz))).astype(o_ref.dtype)
            if tt < chunk:
                o_ref[:, h * GDN_DV:(h + 1) * GDN_DV] = out[:tt]
            else:
                o_ref[c * chunk:(c + 1) * chunk, h * GDN_DV:(h + 1) * GDN_DV] = out


def _gdn(proj, ba, prev, s0, w_conv, a_log, dt_bias, gdn_norm, batch, seq, heads, qkv_block, z_block,
         n_valid, precise):
    kw = heads * GDN_DK
    vw = heads * GDN_DV
    cw = 2 * kw + vw
    chunk = GDN_CHUNK
    tt = _tile(seq, 128)
    assert tt % chunk == 0 or tt < chunk
    nt = seq // tt
    bat = jnp.swapaxes(ba[:, :2 * SUBLANES].reshape(batch, seq, 2 * SUBLANES), 1, 2)
    if seq < chunk:
        bat = jnp.pad(bat, ((0, 0), (0, 0), (0, chunk - seq)))
    arow = jnp.zeros((1, LANES), F32).at[0, heads:2 * heads].set(a_log.astype(F32))
    dtrow = jnp.zeros((1, LANES), F32).at[0, heads:2 * heads].set(dt_bias.astype(F32))
    acol = arow[0, :2 * SUBLANES].reshape(2 * SUBLANES, 1)
    dtcol = dtrow[0, :2 * SUBLANES].reshape(2 * SUBLANES, 1)
    const = lambda b, t: (0, 0)
    o, s_new = pl.pallas_call(
        functools.partial(_gdn_body, heads=heads, chunk=chunk, n_valid=n_valid, precise=precise),
        out_shape=(jax.ShapeDtypeStruct((batch * seq, vw), BF16),
                   jax.ShapeDtypeStruct((batch, heads, GDN_DK, GDN_DV), F32)),
        grid=(batch, nt),
        in_specs=[
            pl.BlockSpec((tt, cw), lambda b, t: (b * nt + t, qkv_block)),
            pl.BlockSpec((tt, vw), lambda b, t: (b * nt + t, z_block)),
            pl.BlockSpec((tt, LANES), lambda b, t: (b * nt + t, 0)),
            pl.BlockSpec((None, 2 * SUBLANES, max(tt, chunk)), lambda b, t: (b, 0, t)),
            pl.BlockSpec((None, SUBLANES, cw), lambda b, t: (b, 0, 0)),
            pl.BlockSpec((None, heads, GDN_DK, GDN_DV), lambda b, t: (b, 0, 0, 0)),
            pl.BlockSpec((CONV_W, cw), const),
            pl.BlockSpec((1, LANES), const),
            pl.BlockSpec((1, LANES), const),
            pl.BlockSpec((2 * SUBLANES, 1), const),
            pl.BlockSpec((2 * SUBLANES, 1), const),
            pl.BlockSpec((1, GDN_DV), const),
        ],
        out_specs=(pl.BlockSpec((tt, vw), lambda b, t: (b * nt + t, 0)),
                   pl.BlockSpec((None, heads, GDN_DK, GDN_DV), lambda b, t: (b, 0, 0, 0))),
        scratch_shapes=[pltpu.VMEM((SUBLANES, cw), F32)],
        compiler_params=_params("parallel", "arbitrary"),
        name="gdn",
    )(proj, proj, ba, bat, prev, s0, w_conv.astype(F32), arow, dtrow, acol, dtcol,
      gdn_norm.reshape(1, GDN_DV).astype(F32))
    return o, s_new


def _mem_attn_body(q_ref, k_ref, v_ref, g_ref, o_ref, *, heads, dh):
    scale = dh ** -0.5
    for h in range(heads):
        hs = slice(h * dh, (h + 1) * dh)
        q = q_ref[:, hs]
        q = q * lax.rsqrt(jnp.mean(q * q, axis=-1, keepdims=True) + EPS) * g_ref[...]
        s = _dot_nt(q, k_ref[:, hs]) * scale
        e = jnp.exp(s - jnp.max(s, axis=-1, keepdims=True))
        p = e / jnp.sum(e, axis=-1, keepdims=True)
        o_ref[:, hs] = _dot(p, v_ref[:, hs]).astype(o_ref.dtype)


def _mem_attn(proj, mem_k, mem_v, q_norm, batch, seq, q_block):
    n_mem, heads, dh = mem_k.shape[1:]
    w = heads * dh
    tt = _tile(seq, 512)
    nt = seq // tt
    return pl.pallas_call(
        functools.partial(_mem_attn_body, heads=heads, dh=dh),
        out_shape=jax.ShapeDtypeStruct((batch * seq, w), BF16),
        grid=(batch, nt),
        in_specs=[
            pl.BlockSpec((tt, w), lambda b, t: (b * nt + t, q_block)),
            pl.BlockSpec((None, n_mem, w), lambda b, t: (b, 0, 0)),
            pl.BlockSpec((None, n_mem, w), lambda b, t: (b, 0, 0)),
            pl.BlockSpec((1, dh), lambda b, t: (0, 0)),
        ],
        out_specs=pl.BlockSpec((tt, w), lambda b, t: (b * nt + t, 0)),
        compiler_params=_params("parallel", "parallel"),
        name="mem_attn",
    )(proj, mem_k.reshape(batch, n_mem, w), mem_v.reshape(batch, n_mem, w), q_norm.reshape(1, dh).astype(F32))


def _merge_body(a0_ref, a1_ref, a2_ref, w0_ref, w1_ref, w2_ref, g0_ref, g1_ref, g2_ref, o_ref):
    acc = _sigmoid(g0_ref[...]) * jnp.dot(a0_ref[...], w0_ref[...], preferred_element_type=F32)
    acc += _sigmoid(g1_ref[...]) * jnp.dot(a1_ref[...], w1_ref[...], preferred_element_type=F32)
    acc += _sigmoid(g2_ref[...]) * jnp.dot(a2_ref[...], w2_ref[...], preferred_element_type=F32)
    o_ref[...] = acc.astype(o_ref.dtype)


def _merge(branches, weights, proj, gate_col, d_model):
    m = proj.shape[0]
    tm = _tile(m, 1024)
    tn = _tile(d_model, 512)
    a_specs = [pl.BlockSpec((tm, a.shape[1]), lambda i, j: (i, 0)) for a in branches]
    w_specs = [pl.BlockSpec((w.shape[0], tn), lambda i, j: (0, j)) for w in weights]
    g_specs = [pl.BlockSpec((tm, tn), functools.partial(lambda i, j, off: (i, off + j), off=(gate_col + g * d_model) // tn))
               for g in range(3)]
    return pl.pallas_call(
        _merge_body,
        out_shape=jax.ShapeDtypeStruct((m, d_model), BF16),
        grid=(m // tm, d_model // tn),
        in_specs=a_specs + w_specs + g_specs,
        out_specs=pl.BlockSpec((tm, tn), lambda i, j: (i, j)),
        compiler_params=_params("parallel", "arbitrary"),
        name="merge",
    )(*branches, *weights, proj, proj, proj)


def _ffn_body(x_ref, g_ref, wg_ref, wu_ref, wo_ref, o_ref, h_ref, *, splits):
    f = pl.program_id(1)

    @pl.when(f == 0)
    def _():
        x = x_ref[...]
        ms = jnp.mean(x * x, axis=-1, keepdims=True)
        h_ref[...] = (x * lax.rsqrt(ms + EPS) * g_ref[...]).astype(BF16)
        o_ref[...] = x

    h = h_ref[...]
    gate = jnp.dot(h, wg_ref[...], preferred_element_type=F32)
    up = jnp.dot(h, wu_ref[...], preferred_element_type=F32)
    act = (gate * _sigmoid(gate) * up).astype(BF16)
    w = o_ref.shape[1] // splits
    for s in range(splits):
        o_ref[:, s * w:(s + 1) * w] += jnp.dot(act, wo_ref[:, s * w:(s + 1) * w], preferred_element_type=F32)


def _ffn(x, g, w_in, w_out, splits=4):
    m, d = x.shape
    dff = w_out.shape[0]
    tm = _tile(m, 512)
    tf = _tile(dff, 512)
    nf = dff // tf
    return pl.pallas_call(
        functools.partial(_ffn_body, splits=splits),
        out_shape=jax.ShapeDtypeStruct((m, d), F32),
        grid=(m // tm, nf),
        in_specs=[
            pl.BlockSpec((tm, d), lambda i, f: (i, 0)),
            pl.BlockSpec((1, d), lambda i, f: (0, 0)),
            pl.BlockSpec((d, tf), lambda i, f: (0, f)),
            pl.BlockSpec((d, tf), lambda i, f: (0, nf + f)),
            pl.BlockSpec((tf, d), lambda i, f: (f, 0)),
        ],
        out_specs=pl.BlockSpec((tm, d), lambda i, f: (i, 0)),
        scratch_shapes=[pltpu.VMEM((tm, d), BF16)],
        compiler_params=_params("parallel", "arbitrary"),
        name="ffn",
    )(x, g.reshape(1, d).astype(F32), w_in, w_in, w_out)


def _trunk(x, p, sb_attend, conv_prev, s0, mem_k, mem_v, n_valid, precise_gdn):
    batch, seq, d = x.shape
    heads = p["sb_heads"]
    x2 = x.reshape(batch * seq, d)
    h = _rmsnorm(x2, p["ln_mix"])
    proj, ba = _matmul_with_side(h, p["w_main"], p["w_ba"])
    o_sb = sb_attend(proj)
    o_gdn, s_new = _gdn(proj, ba, conv_prev, s0, p["w_conv"], p["a_log"], p["dt_bias"], p["gdn_norm"],
                        batch, seq, p["gdn_heads"], qkv_block=1, z_block=p["z_block"],
                        n_valid=n_valid, precise=precise_gdn)
    o_mem = _mem_attn(proj, mem_k, mem_v, p["q_norm_mem"], batch, seq, q_block=p["memq_block"])
    merged = _merge([o_sb, o_gdn, o_mem], [p["w_proj_sb"], p["w_proj_gdn"], p["w_proj_mem"]],
                    proj, p["gate_col"], d)
    x1 = _matmul(merged, p["w_out"], res=x2)
    y = _ffn(x1, p["ln_ffn"], p["w_ffn_in"], p["w_ffn_out"])
    return y.reshape(batch, seq, d), proj, s_new


def kernel(x_prompt, x_sample, mem_prompt, cache_sb_k, cache_sb_v, state_gdn, state_conv, cache_mem_k, cache_mem_v, page_table, ln_mix, w_in, sb_bias, w_conv, a_log, dt_bias, gdn_norm, q_norm_mem, k_norm_mem, ln_mem, w_mem_kv, w_proj_sb, w_proj_gdn, w_proj_mem, w_out, ln_ffn, w_ffn_in, w_ffn_out):
    depth = w_in.shape[0]
    d_model = x_prompt.shape[-1]
    bp, tp = x_prompt.shape[:2]
    bs, ts = x_sample.shape[:2]
    sb_heads = sb_bias.shape[-1]
    gdn_heads = a_log.shape[-1]
    mem_heads, mem_dim = cache_mem_k.shape[-2:]
    n_mem = mem_prompt.shape[1]
    sb_w = sb_heads * SB_DIM
    kw = gdn_heads * GDN_DK
    vw = gdn_heads * GDN_DV
    cw = 2 * kw + vw
    mem_w = mem_heads * mem_dim
    assert sb_w == kw == vw == mem_w and cw == 3 * sb_w and 2 * gdn_heads <= 2 * SUBLANES
    off_ba = 3 * sb_w + cw + vw
    off_memq = off_ba + 2 * gdn_heads
    main_w = off_ba + mem_w + 3 * d_model
    assert w_in.shape[-1] == off_memq + mem_w + 3 * d_model and main_w % LANES == 0

    yp = x_prompt
    ys = jnp.pad(x_sample, ((0, 0), (0, SAMPLE_PAD - ts), (0, 0)))
    outs = [[] for _ in range(10)]
    for l in range(depth):
        w_main = jnp.concatenate([w_in[l][:, :off_ba], w_in[l][:, off_memq:]], axis=1).astype(BF16)
        w_ba = jnp.pad(w_in[l][:, off_ba:off_memq], ((0, 0), (0, LANES - 2 * gdn_heads))).astype(BF16)
        p = {
            "sb_heads": sb_heads, "gdn_heads": gdn_heads,
            "z_block": (3 * sb_w + cw) // vw, "memq_block": off_ba // mem_w, "gate_col": off_ba + mem_w,
            "ln_mix": ln_mix[l], "w_main": w_main, "w_ba": w_ba, "w_conv": w_conv[l], "a_log": a_log[l],
            "dt_bias": dt_bias[l], "gdn_norm": gdn_norm[l], "q_norm_mem": q_norm_mem[l],
            "w_proj_sb": w_proj_sb[l].astype(BF16), "w_proj_gdn": w_proj_gdn[l].astype(BF16),
            "w_proj_mem": w_proj_mem[l].astype(BF16), "w_out": w_out[l].astype(BF16), "ln_ffn": ln_ffn[l],
            "w_ffn_in": w_ffn_in[l].astype(BF16), "w_ffn_out": w_ffn_out[l].astype(BF16),
        }

        kv = _matmul(_rmsnorm(mem_prompt.reshape(bp * n_mem, d_model), ln_mem[l]), w_mem_kv[l].astype(BF16))
        mk = _headnorm(kv, k_norm_mem[l], mem_heads, 0).reshape(bp, n_mem, mem_heads, mem_dim)
        mv = kv[:, mem_w:].reshape(bp, n_mem, mem_heads, mem_dim)
        yp, proj_p, sp = _trunk(
            yp, p, lambda proj: _sb_prompt(proj, sb_bias[l], bp, tp, sb_heads),
            jnp.zeros((bp, SUBLANES, cw), F32), jnp.zeros((bp, gdn_heads, GDN_DK, GDN_DV), F32),
            mk, mv, n_valid=None, precise_gdn=False)

        conv_prev = jnp.pad(state_conv[l], ((0, 0), (SUBLANES - (CONV_W - 1), 0), (0, 0)))
        ys, proj_s, ss = _trunk(
            ys, p, lambda proj: _sb_sample(proj, sb_bias[l], cache_sb_k, cache_sb_v, page_table, l, sb_heads),
            conv_prev, state_gdn[l], cache_mem_k[l], cache_mem_v[l], n_valid=ts, precise_gdn=True)

        pp = proj_p.reshape(bp, tp, main_w)
        ps = proj_s.reshape(bs, SAMPLE_PAD, main_w)
        outs[0].append(pp[:, :, sb_w:2 * sb_w].reshape(bp, tp, sb_heads, SB_DIM))
        outs[1].append(pp[:, :, 2 * sb_w:3 * sb_w].reshape(bp, tp, sb_heads, SB_DIM))
        outs[2].append(ps[:, :ts, sb_w:2 * sb_w].reshape(bs, ts, sb_heads, SB_DIM))
        outs[3].append(ps[:, :ts, 2 * sb_w:3 * sb_w].reshape(bs, ts, sb_heads, SB_DIM))
        outs[4].append(sp)
        outs[5].append(ss)
        outs[6].append(pp[:, tp - (CONV_W - 1):, 3 * sb_w:3 * sb_w + cw])
        outs[7].append(ps[:, ts - (CONV_W - 1):ts, 3 * sb_w:3 * sb_w + cw])
        outs[8].append(mk)
        outs[9].append(mv)
    return (yp, ys[:, :ts]) + tuple(jnp.stack(o) for o in outs)
```
